```python
import jax, jax.numpy as jnp
from jax import lax
import numpy as np

D_MODEL = 4096
BATCH = 32
SEQ = 256
DEPTH = 2
DEC_BATCH = 2
DEC_SEQ = 4096
PAST_LEN = 256

GRID_W = 64
N_MIXERS = 2
N_EVEN = (DEPTH + 1) // 2
N_ODD = DEPTH // 2
N_MOD = 6
FNET_GROUPS = 8
N_HEADS = 32
Q_RANK = 1024
KV_RANK = 512
NOPE_DIM = 128
ROPE_DIM = 64
V_DIM = 128
ROPE_THETA = 10000.0
D_FF = 14336
N_EXPERTS = 8
TOP_K = 2
D_EXPERT = 14336
Q_BLOCK = 128
EPS = 1e-6

kernel_name = 'fnet_mla_interleaved_diffusion_step'


def rms_norm(x, g):
    xf = x.astype(jnp.float32)
    y = xf * lax.rsqrt(jnp.mean(xf * xf, axis=-1, keepdims=True) + EPS)
    return (y * g.astype(jnp.float32)).astype(x.dtype)


def adaln_params(cond, w, b):
    m = jax.nn.silu(cond) @ w + b
    return jnp.split(m[:, None, :], N_MOD, axis=-1)


def modulate(h, shift, scale):
    return h * (1 + scale) + shift


def axial_rope_tables(n_tokens):
    rows = n_tokens // GRID_W
    row = jnp.repeat(jnp.arange(rows, dtype=jnp.float32), GRID_W)
    col = jnp.tile(jnp.arange(GRID_W, dtype=jnp.float32), rows)
    half = ROPE_DIM // 2
    inv = ROPE_THETA ** (-jnp.arange(0, half, 2, dtype=jnp.float32) / half)
    ang_r = row[:, None] * inv
    ang_c = col[:, None] * inv
    return (jnp.cos(ang_r), jnp.sin(ang_r), jnp.cos(ang_c), jnp.sin(ang_c))


def rotate(x, cos, sin):
    x1, x2 = jnp.split(x, 2, axis=-1)
    return jnp.concatenate([x1 * cos - x2 * sin, x2 * cos + x1 * sin], axis=-1)


def apply_axial_rope(x, cos_r, sin_r, cos_c, sin_c):
    xf = x.astype(jnp.float32)
    xr, xc = jnp.split(xf, 2, axis=-1)
    return jnp.concatenate([rotate(xr, cos_r, sin_r), rotate(xc, cos_c, sin_c)], axis=-1).astype(x.dtype)


def fourier_mixer(h, w_out):
    b, t, d = h.shape
    hg = h.astype(jnp.float32).reshape(b, t, FNET_GROUPS, d // FNET_GROUPS)
    f = jnp.fft.fft2(hg, axes=(1, 3), norm='ortho').real
    return f.reshape(b, t, d).astype(h.dtype) @ w_out


def mla_queries(h, w_q_a, g_q_a, w_q_b):
    b, t, _ = h.shape
    q = (rms_norm(h @ w_q_a, g_q_a) @ w_q_b).reshape(b, t, N_HEADS, NOPE_DIM + ROPE_DIM)
    return q[..., :NOPE_DIM], q[..., NOPE_DIM:]


def mla_compress_kv(h, w_kv_a, g_kv_a):
    kv = h @ w_kv_a
    return rms_norm(kv[..., :KV_RANK], g_kv_a), kv[..., KV_RANK:]


def mla_attend(q_nope, q_rope, ckv, k_rope, w_kv_b):
    b, s, _ = ckv.shape
    kv = (ckv @ w_kv_b).reshape(b, s, N_HEADS, NOPE_DIM + V_DIM)
    k_nope, v = kv[..., :NOPE_DIM], kv[..., NOPE_DIM:]
    t = q_nope.shape[1]
    nb = t // Q_BLOCK
    scale = (NOPE_DIM + ROPE_DIM) ** -0.5

    def block(qs):
        qn, qr = qs
        sc = (jnp.einsum('bqhd,bshd->bhqs', qn, k_nope, preferred_element_type=jnp.float32)
              + jnp.einsum('bqhr,bsr->bhqs', qr, k_rope, preferred_element_type=jnp.float32))
        p = jax.nn.softmax(sc * scale, axis=-1).astype(v.dtype)
        return jnp.einsum('bhqs,bshv->bqhv', p, v)

    qn_b = q_nope.reshape(b, nb, Q_BLOCK, N_HEADS, NOPE_DIM).swapaxes(0, 1)
    qr_b = q_rope.reshape(b, nb, Q_BLOCK, N_HEADS, ROPE_DIM).swapaxes(0, 1)
    out = lax.map(block, (qn_b, qr_b))
    return out.swapaxes(0, 1).reshape(b, t, N_HEADS * V_DIM)


def swiglu(x, wg, wu, wd):
    return (jax.nn.silu(x @ wg) * (x @ wu)) @ wd


def moe_swiglu(h, w_router, wg, wu, wd):
    b, t, d = h.shape
    x = h.reshape(b * t, d)
    logits = (x @ w_router).astype(jnp.float32)
    top_v, top_i = lax.top_k(logits, TOP_K)
    gates = jax.nn.softmax(top_v, axis=-1)
    combine = jnp.sum(jax.nn.one_hot(top_i, N_EXPERTS, dtype=jnp.float32) * gates[..., None], axis=1)
    out = jnp.zeros_like(x)
    for e in range(N_EXPERTS):
        out = out + combine[:, e:e + 1].astype(x.dtype) * swiglu(x, wg[e], wu[e], wd[e])
    return out.reshape(b, t, d)


def setup_inputs(seed: int = 0) -> dict:
    key = jax.random.key(seed)
    ks = jax.random.split(key, 26)
    f32 = jnp.float32

    def nrm(k, shape, fan_in):
        return jax.random.normal(k, shape, f32) * (fan_in ** -0.5)

    def gain(k, shape):
        return 1.0 + 0.02 * jax.random.normal(k, shape, f32)

    return {
        'x_prompt': jax.random.normal(ks[0], (BATCH, SEQ, D_MODEL), f32),
        'x_sample': jax.random.normal(ks[1], (DEC_BATCH, DEC_SEQ, D_MODEL), f32),
        'cache_ckv': jax.random.normal(ks[2], (DEC_BATCH, N_ODD, PAST_LEN, KV_RANK), f32),
        'cache_krope': jax.random.normal(ks[3], (DEC_BATCH, N_ODD, PAST_LEN, ROPE_DIM), f32),
        'c': jax.random.normal(ks[4], (DEC_BATCH, D_MODEL), f32),
        'c_ctx': jax.random.normal(ks[5], (D_MODEL,), f32),
        'w_mod': 0.5 * nrm(ks[6], (DEPTH, D_MODEL, N_MOD * D_MODEL), D_MODEL),
        'b_mod': 0.02 * jax.random.normal(ks[7], (DEPTH, N_MOD * D_MODEL), f32),
        'g_mix': gain(ks[8], (DEPTH, D_MODEL)),
        'g_ffn': gain(ks[9], (DEPTH, D_MODEL)),
        'w_fnet_out': nrm(ks[10], (N_EVEN, D_MODEL, D_MODEL), D_MODEL),
        'w_ffn_gate': nrm(ks[11], (N_EVEN, D_MODEL, D_FF), D_MODEL),
        'w_ffn_up': nrm(ks[12], (N_EVEN, D_MODEL, D_FF), D_MODEL),
        'w_ffn_down': nrm(ks[13], (N_EVEN, D_FF, D_MODEL), D_FF),
        'w_q_a': nrm(ks[14], (N_ODD, D_MODEL, Q_RANK), D_MODEL),
        'g_q_a': gain(ks[15], (N_ODD, Q_RANK)),
        'w_q_b': nrm(ks[16], (N_ODD, Q_RANK, N_HEADS * (NOPE_DIM + ROPE_DIM)), Q_RANK),
        'w_kv_a': nrm(ks[17], (N_ODD, D_MODEL, KV_RANK + ROPE_DIM), D_MODEL),
        'g_kv_a': gain(ks[18], (N_ODD, KV_RANK)),
        'w_kv_b': nrm(ks[19], (N_ODD, KV_RANK, N_HEADS * (NOPE_DIM + V_DIM)), KV_RANK),
        'w_o': nrm(ks[20], (N_ODD, N_HEADS * V_DIM, D_MODEL), N_HEADS * V_DIM),
        'w_router': nrm(ks[21], (N_ODD, D_MODEL, N_EXPERTS), D_MODEL),
        'w_exp_gate': nrm(ks[22], (N_ODD, N_EXPERTS, D_MODEL, D_EXPERT), D_MODEL),
        'w_exp_up': nrm(ks[23], (N_ODD, N_EXPERTS, D_MODEL, D_EXPERT), D_MODEL),
        'w_exp_down': nrm(ks[24], (N_ODD, N_EXPERTS, D_EXPERT, D_MODEL), D_EXPERT),
        'g_final': gain(ks[25], (D_MODEL,)),
    }


def reference(x_prompt, x_sample, cache_ckv, cache_krope, c, c_ctx, w_mod, b_mod, g_mix, g_ffn,
              w_fnet_out, w_ffn_gate, w_ffn_up, w_ffn_down, w_q_a, g_q_a, w_q_b, w_kv_a, g_kv_a,
              w_kv_b, w_o, w_router, w_exp_gate, w_exp_up, w_exp_down, g_final):
    xp, xs = x_prompt, x_sample
    rope_k = axial_rope_tables(xs.shape[1])
    rope_q = tuple(a[:, None, :] for a in rope_k)
    new_ckv, new_krope = [], []
    for i in range(DEPTH):
        j = i // N_MIXERS
        sh_mp, sc_mp, gt_mp, sh_fp, sc_fp, gt_fp = adaln_params(c_ctx[None, :], w_mod[i], b_mod[i])
        sh_ms, sc_ms, gt_ms, sh_fs, sc_fs, gt_fs = adaln_params(c, w_mod[i], b_mod[i])
        hp = modulate(rms_norm(xp, g_mix[i]), sh_mp, sc_mp)
        hs = modulate(rms_norm(xs, g_mix[i]), sh_ms, sc_ms)
        if i % N_MIXERS == 0:
            mix_p = fourier_mixer(hp, w_fnet_out[j])
            mix_s = fourier_mixer(hs, w_fnet_out[j])
        else:
            qn_p, qr_p = mla_queries(hp, w_q_a[j], g_q_a[j], w_q_b[j])
            ckv_p, kr_p = mla_compress_kv(hp, w_kv_a[j], g_kv_a[j])
            new_ckv.append(ckv_p)
            new_krope.append(kr_p)
            mix_p = mla_attend(qn_p, qr_p, ckv_p, kr_p, w_kv_b[j]) @ w_o[j]
            qn_s, qr_s = mla_queries(hs, w_q_a[j], g_q_a[j], w_q_b[j])
            qr_s = apply_axial_rope(qr_s, *rope_q)
            ckv_s, kr_s = mla_compress_kv(hs, w_kv_a[j], g_kv_a[j])
            kr_s = apply_axial_rope(kr_s, *rope_k)
            ckv_all = jnp.concatenate([cache_ckv[:, j], ckv_s], axis=1)
            kr_all = jnp.concatenate([cache_krope[:, j], kr_s], axis=1)
            mix_s = mla_attend(qn_s, qr_s, ckv_all, kr_all, w_kv_b[j]) @ w_o[j]
        xp = xp + gt_mp * mix_p
        xs = xs + gt_ms * mix_s
        hp = modulate(rms_norm(xp, g_ffn[i]), sh_fp, sc_fp)
        hs = modulate(rms_norm(xs, g_ffn[i]), sh_fs, sc_fs)
        if i % N_MIXERS == 0:
            ff_p = swiglu(hp, w_ffn_gate[j], w_ffn_up[j], w_ffn_down[j])
            ff_s = swiglu(hs, w_ffn_gate[j], w_ffn_up[j], w_ffn_down[j])
        else:
            ff_p = moe_swiglu(hp, w_router[j], w_exp_gate[j], w_exp_up[j], w_exp_down[j])
            ff_s = moe_swiglu(hs, w_router[j], w_exp_gate[j], w_exp_up[j], w_exp_down[j])
        xp = xp + gt_fp * ff_p
        xs = xs + gt_fs * ff_s
    y_prompt = rms_norm(xp, g_final)
    y_sample = rms_norm(xs, g_final)
    new_cache_ckv = jnp.stack(new_ckv, axis=1)
    new_cache_krope = jnp.stack(new_krope, axis=1)
    return (y_prompt, y_sample, new_cache_ckv, new_cache_krope)
```

```python
import functools
import math

import jax
import jax.numpy as jnp
from jax import lax
from jax.experimental import pallas as pl
from jax.experimental.pallas import tpu as pltpu

GRID_W = 64
FNET_GROUPS = 8
N_HEADS = 32
NOPE_DIM = 128
ROPE_DIM = 64
V_DIM = 128
ROPE_THETA = 10000.0
TOP_K = 2
N_MOD = 6
EPS = 1e-6
HEADS_PER_STEP = 2
COND_ROWS = 8
ROUTER_LANES = 128
V7X_VMEM_LIMIT_BYTES = 60 * 1024 * 1024

BF16 = jnp.bfloat16
F32 = jnp.float32


def _params(n_axes=None):
    return pltpu.CompilerParams(vmem_limit_bytes=V7X_VMEM_LIMIT_BYTES)


def _tile(n, pref):
    t = min(n, pref)
    while n % t:
        t -= 1
    return t


def _bf16_dot(a, b):
    return jnp.dot(a, b, preferred_element_type=F32)


def _to_bf16(v):
    return v if v.dtype == BF16 else v.astype(BF16)


def _seg_of_tile(i, bm, mp, ts):
    r = i * bm
    return jnp.where(r < mp, 0, 1 + (r - mp) // ts)


def _adaln_kernel(c_ref, w_ref, b_ref, o_ref):
    c = c_ref[...]
    s = c * (1.0 / (1.0 + jnp.exp(-c)))
    o_ref[...] = _bf16_dot(s.astype(BF16), w_ref[...].astype(BF16)) + b_ref[...]


def adaln(cond, w_mod, b_mod, layer):
    r, d = cond.shape
    n = w_mod.shape[-1]
    bn = _tile(n, 1024)
    out = pl.pallas_call(
        _adaln_kernel,
        grid=(n // bn,),
        in_specs=[
            pl.BlockSpec((r, d), lambda j: (0, 0)),
            pl.BlockSpec((None, d, bn), lambda j: (layer, 0, j)),
            pl.BlockSpec((None, 1, bn), lambda j: (layer, 0, j)),
        ],
        out_specs=pl.BlockSpec((r, bn), lambda j: (0, j)),
        out_shape=jax.ShapeDtypeStruct((r, n), F32),
        compiler_params=_params(),
        name="adaln",
    )(cond, w_mod, b_mod.reshape(b_mod.shape[0], 1, n))
    return out.reshape(r * N_MOD, 1, d)


def _pack_bf16_pair(lo, hi):
    lo_bits = lax.bitcast_convert_type(lo.astype(BF16).astype(F32), jnp.uint32)
    hi_bits = lax.bitcast_convert_type(hi.astype(BF16).astype(F32), jnp.uint32)
    return (lo_bits >> 16) | hi_bits


def _unpack_bf16_pair(w):
    lo = lax.bitcast_convert_type(w << 16, F32).astype(BF16)
    hi = lax.bitcast_convert_type(w & jnp.uint32(0xFFFF0000), F32).astype(BF16)
    return lo, hi


def _norm_mod_kernel(x_ref, g_ref, sh_ref, sc_ref, o_ref, *, packed):
    x = x_ref[...]
    y = x * lax.rsqrt(jnp.mean(x * x, axis=-1, keepdims=True) + EPS)
    y = y * g_ref[...]
    h = y * (1.0 + sc_ref[...]) + sh_ref[...]
    if packed:
        half = h.shape[-1] // 2
        o_ref[...] = _pack_bf16_pair(h[:, :half], h[:, half:])
    else:
        o_ref[...] = h.astype(o_ref.dtype)


def norm_mod(x, g, mod, k_shift, k_scale, *, mp, ts, packed):
    m, d = x.shape
    bm = _tile(math.gcd(mp, ts), 256)
    seg = functools.partial(_seg_of_tile, bm=bm, mp=mp, ts=ts)
    out_d, out_dtype = (d // 2, jnp.uint32) if packed else (d, BF16)
    return pl.pallas_call(
        functools.partial(_norm_mod_kernel, packed=packed),
        grid=(m // bm,),
        in_specs=[
            pl.BlockSpec((bm, d), lambda i: (i, 0)),
            pl.BlockSpec((1, d), lambda i: (0, 0)),
            pl.BlockSpec((None, 1, d), lambda i: (seg(i) * N_MOD + k_shift, 0, 0)),
            pl.BlockSpec((None, 1, d), lambda i: (seg(i) * N_MOD + k_scale, 0, 0)),
        ],
        out_specs=pl.BlockSpec((bm, out_d), lambda i: (i, 0)),
        out_shape=jax.ShapeDtypeStruct((m, out_d), out_dtype),
        compiler_params=_params(),
        name="norm_mod",
    )(x, g.reshape(1, d), mod, mod)


def _mm_kernel(a_ref, w_ref, o_ref):
    o_ref[...] = _bf16_dot(a_ref[...], _to_bf16(w_ref[...])).astype(o_ref.dtype)


def matmul(a, w, out_dtype, *, bm, bn, name):
    m, k = a.shape
    n = w.shape[1]
    bm, bn = _tile(m, bm), _tile(n, bn)
    return pl.pallas_call(
        _mm_kernel,
        grid=(m // bm, n // bn),
        in_specs=[
            pl.BlockSpec((bm, k), lambda i, j: (i, 0)),
            pl.BlockSpec((k, bn), lambda i, j: (0, j)),
        ],
        out_specs=pl.BlockSpec((bm, bn), lambda i, j: (i, j)),
        out_shape=jax.ShapeDtypeStruct((m, n), out_dtype),
        compiler_params=_params(),
        name=name,
    )(a, w)


def _mm_res_kernel(a_ref, w_ref, res_ref, gate_ref, o_ref):
    acc = _bf16_dot(a_ref[...], _to_bf16(w_ref[...]))
    o_ref[...] = res_ref[...] + gate_ref[...] * acc


def matmul_residual(a, w, res, mod, k_gate, *, mp, ts, name):
    m, k = a.shape
    n = w.shape[1]
    bm = _tile(math.gcd(mp, ts), 1024)
    bn = _tile(n, 512)
    seg = functools.partial(_seg_of_tile, bm=bm, mp=mp, ts=ts)
    return pl.pallas_call(
        _mm_res_kernel,
        grid=(m // bm, n // bn),
        in_specs=[
            pl.BlockSpec((bm, k), lambda i, j: (i, 0)),
            pl.BlockSpec((k, bn), lambda i, j: (0, j)),
            pl.BlockSpec((bm, bn), lambda i, j: (i, j)),
            pl.BlockSpec((None, 1, bn), lambda i, j: (seg(i) * N_MOD + k_gate, 0, j)),
        ],
        out_specs=pl.BlockSpec((bm, bn), lambda i, j: (i, j)),
        out_shape=jax.ShapeDtypeStruct((m, n), F32),
        compiler_params=_params(),
        name=name,
    )(a, w, res, mod)


def _dft_matrices(n):
    j = jnp.arange(n, dtype=jnp.int32)
    jk = (j[:, None] * j[None, :]) % n
    ang = jk.astype(F32) * (2.0 * math.pi / n)
    s = n ** -0.5
    return jnp.cos(ang) * s, jnp.sin(ang) * s


def _dft_channel_kernel(h_ref, w_ref, o_ref):
    c = h_ref.shape[-1]
    y = _bf16_dot(h_ref[...], w_ref[...])
    o_ref[0] = y[:, :c].astype(o_ref.dtype)
    o_ref[1] = y[:, c:].astype(o_ref.dtype)


def dft_channels(h, w_cs, *, row0, nseq, t):
    d = h.shape[1]
    c = d // FNET_GROUPS
    bm = _tile(t, 1024)
    tb = t // bm
    return pl.pallas_call(
        _dft_channel_kernel,
        grid=(nseq * tb, FNET_GROUPS),
        in_specs=[
            pl.BlockSpec((bm, c), lambda i, g: (row0 // bm + i, g)),
            pl.BlockSpec((c, 2 * c), lambda i, g: (0, 0)),
        ],
        out_specs=pl.BlockSpec((None, 2, bm, c), lambda i, g: (i // tb, 0, i % tb, g)),
        out_shape=jax.ShapeDtypeStruct((nseq, 2, t, d), BF16),
        compiler_params=_params(),
        name="dft_channels",
    )(h, w_cs)


def _bmm_kernel(a_ref, b_ref, o_ref):
    o_ref[...] = _bf16_dot(a_ref[...], b_ref[...]).astype(o_ref.dtype)


def dft_positions(a_cs, y):
    nseq, k, d = y.shape
    t = a_cs.shape[0]
    bm, bn = _tile(t, 512), _tile(d, 512)
    tb = t // bm
    return pl.pallas_call(
        _bmm_kernel,
        grid=(nseq, tb, d // bn),
        in_specs=[
            pl.BlockSpec((bm, k), lambda b, i, j: (i, 0)),
            pl.BlockSpec((None, k, bn), lambda b, i, j: (b, 0, j)),
        ],
        out_specs=pl.BlockSpec((bm, bn), lambda b, i, j: (b * tb + i, j)),
        out_shape=jax.ShapeDtypeStruct((nseq * t, d), BF16),
        compiler_params=_params(),
        name="dft_positions",
    )(a_cs, y)


def fourier_mix(h, *, mp, tp, ts):
    m, d = h.shape
    c = d // FNET_GROUPS
    cc, sc = _dft_matrices(c)
    w_cs = jnp.concatenate([cc, sc], axis=1).astype(BF16)
    outs = []
    for row0, rows, t in ((0, mp, tp), (mp, m - mp, ts)):
        nseq = rows // t
        ct, st = _dft_matrices(t)
        a_cs = jnp.concatenate([ct, -st], axis=1).astype(BF16)
        y = dft_channels(h, w_cs, row0=row0, nseq=nseq, t=t)
        outs.append(dft_positions(a_cs, y.reshape(nseq, 2 * t, d)))
    return jnp.concatenate(outs, axis=0)


def _ffn_kernel(te_ref, tv_ref, x_ref, wg_ref, wu_ref, wd_ref, o_ref, *, col_chunk):
    del te_ref
    i, f = pl.program_id(0), pl.program_id(1)

    @pl.when(f == 0)
    def _():
        o_ref[...] = jnp.zeros_like(o_ref)

    @pl.when(tv_ref[i] > 0)
    def _():
        x = x_ref[...]
        g = _bf16_dot(x, wg_ref[...].astype(BF16))
        u = _bf16_dot(x, wu_ref[...].astype(BF16))
        h = ((g * (1.0 / (1.0 + jnp.exp(-g)))) * u).astype(BF16)
        for c0 in range(0, o_ref.shape[-1], col_chunk):
            cols = slice(c0, c0 + col_chunk)
            o_ref[:, cols] += _bf16_dot(h, wd_ref[:, cols].astype(BF16))


def grouped_swiglu(x, wg, wu, wd, tile_expert, tile_valid, *, bm):
    p, d = x.shape
    ff = wg.shape[-1]
    bf = _tile(ff, 256)
    nf = ff // bf
    n_tiles = p // bm

    def f_eff(i, f, tv):
        return jnp.where(tv[i] > 0, f, nf - 1)

    grid_spec = pltpu.PrefetchScalarGridSpec(
        num_scalar_prefetch=2,
        grid=(n_tiles, nf),
        in_specs=[
            pl.BlockSpec((bm, d), lambda i, f, te, tv: (i, 0), pipeline_mode=pl.Buffered(1)),
            pl.BlockSpec((None, d, bf), lambda i, f, te, tv: (te[i], 0, f_eff(i, f, tv))),
            pl.BlockSpec((None, d, bf), lambda i, f, te, tv: (te[i], 0, f_eff(i, f, tv))),
            pl.BlockSpec((None, bf, d), lambda i, f, te, tv: (te[i], f_eff(i, f, tv), 0)),
        ],
        out_specs=pl.BlockSpec((bm, d), lambda i, f, te, tv: (i, 0), pipeline_mode=pl.Buffered(1)),
    )
    return pl.pallas_call(
        functools.partial(_ffn_kernel, col_chunk=_tile(d, 512)),
        grid_spec=grid_spec,
        out_shape=jax.ShapeDtypeStruct((p, d), F32),
        compiler_params=_params(),
        name="grouped_swiglu",
    )(tile_expert, tile_valid, x, wg, wu, wd)


def _row_copy(src_ref, dst_ref, sem, src_row, dst_row):
    return pltpu.make_async_copy(src_ref.at[pl.ds(src_row, 1)], dst_ref.at[pl.ds(dst_row, 1)], sem)


def _start_row_gather(idx_ref, idx0, idx_stride, src_ref, dst_ref, sem, rows):
    def body(r, carry):
        _row_copy(src_ref, dst_ref, sem, idx_ref[idx0 + r * idx_stride], r).start()
        return carry
    lax.fori_loop(0, rows, body, 0)


def _wait_row_gather(src_ref, dst_ref, sem, rows):
    def body(r, carry):
        _row_copy(src_ref, dst_ref, sem, 0, r).wait()
        return carry
    lax.fori_loop(0, rows, body, 0)


def _gather_unpack_kernel(idx_ref, src_ref, o_ref, buf_ref, sem, *, rows):
    base = pl.program_id(0) * rows
    _start_row_gather(idx_ref, base, 1, src_ref, buf_ref, sem, rows)
    _wait_row_gather(src_ref, buf_ref, sem, rows)
    half = buf_ref.shape[-1]
    lo, hi = _unpack_bf16_pair(buf_ref[...])
    o_ref[:, :half] = lo
    o_ref[:, half:] = hi


def gather_unpack(src, idx):
    n = idx.shape[0]
    half = src.shape[1]
    rows = _tile(n, 512)
    grid_spec = pltpu.PrefetchScalarGridSpec(
        num_scalar_prefetch=1,
        grid=(n // rows,),
        in_specs=[pl.BlockSpec(memory_space=pl.ANY)],
        out_specs=pl.BlockSpec((rows, 2 * half), lambda i, idx: (i, 0)),
        scratch_shapes=[pltpu.VMEM((rows, half), src.dtype), pltpu.SemaphoreType.DMA(())],
    )
    return pl.pallas_call(
        functools.partial(_gather_unpack_kernel, rows=rows),
        grid_spec=grid_spec,
        out_shape=jax.ShapeDtypeStruct((n, 2 * half), BF16),
        compiler_params=_params(),
        name="gather_unpack",
    )(idx, src)


def _gather_combine_kernel(pos_ref, x_ref, gate_ref, w_ref, y_ref, o_ref, buf_ref, sem, *, rows):
    base = pl.program_id(0) * rows * TOP_K
    for k in range(TOP_K):
        _start_row_gather(pos_ref, base + k, TOP_K, y_ref, buf_ref.at[k], sem.at[k], rows)
    for k in range(TOP_K):
        _wait_row_gather(y_ref, buf_ref.at[k], sem.at[k], rows)
    w = w_ref[...]
    ff = w[:, 0:1] * buf_ref[0]
    for k in range(1, TOP_K):
        ff = ff + w[:, k:k + 1] * buf_ref[k]
    o_ref[...] = x_ref[...] + gate_ref[...] * ff


def gather_combine(x, mod, k_gate, y, pos, weights, *, mp, ts):
    m, d = x.shape
    rows = _tile(math.gcd(mp, ts), 256)
    seg = functools.partial(_seg_of_tile, bm=rows, mp=mp, ts=ts)
    row_spec = pl.BlockSpec((rows, d), lambda i, pos: (i, 0))
    grid_spec = pltpu.PrefetchScalarGridSpec(
        num_scalar_prefetch=1,
        grid=(m // rows,),
        in_specs=[
            row_spec,
            pl.BlockSpec((None, 1, d), lambda i, pos: (seg(i) * N_MOD + k_gate, 0, 0)),
            pl.BlockSpec((rows, weights.shape[1]), lambda i, pos: (i, 0)),
            pl.BlockSpec(memory_space=pl.ANY),
        ],
        out_specs=row_spec,
        scratch_shapes=[pltpu.VMEM((TOP_K, rows, d), F32), pltpu.SemaphoreType.DMA((TOP_K,))],
    )
    return pl.pallas_call(
        functools.partial(_gather_combine_kernel, rows=rows),
        grid_spec=grid_spec,
        out_shape=jax.ShapeDtypeStruct((m, d), F32),
        compiler_params=_params(),
        name="gather_combine",
    )(pos, x, mod, weights, y)


def _residual_kernel(x_ref, gate_ref, y_ref, o_ref):
    o_ref[...] = x_ref[...] + gate_ref[...] * y_ref[...]


def gated_residual(x, mod, k_gate, y, *, mp, ts):
    m, d = x.shape
    bm = _tile(math.gcd(mp, ts), 256)
    seg = functools.partial(_seg_of_tile, bm=bm, mp=mp, ts=ts)
    row_spec = pl.BlockSpec((bm, d), lambda i: (i, 0))
    return pl.pallas_call(
        _residual_kernel,
        grid=(m // bm,),
        in_specs=[row_spec, pl.BlockSpec((None, 1, d), lambda i: (seg(i) * N_MOD + k_gate, 0, 0)),
                  row_spec],
        out_specs=row_spec,
        out_shape=jax.ShapeDtypeStruct((m, d), F32),
        compiler_params=_params(),
        name="gated_residual",
    )(x, mod, y)


def _router_kernel(x_ref, w_ref, idx_ref, gate_ref, *, n_experts):
    half = x_ref.shape[-1]
    lo, hi = _unpack_bf16_pair(x_ref[...])
    logits = _bf16_dot(lo, w_ref[:half, :]) + _bf16_dot(hi, w_ref[half:, :])
    lane = lax.broadcasted_iota(jnp.int32, logits.shape, 1)
    neg = jnp.float32(-jnp.inf)
    lg = jnp.where(lane < n_experts, logits, neg)
    m1 = jnp.max(lg, axis=-1, keepdims=True)
    i1 = jnp.min(jnp.where(lg == m1, lane, ROUTER_LANES), axis=-1, keepdims=True)
    lg2 = jnp.where(lane == i1, neg, lg)
    m2 = jnp.max(lg2, axis=-1, keepdims=True)
    i2 = jnp.min(jnp.where(lg2 == m2, lane, ROUTER_LANES), axis=-1, keepdims=True)
    e = jnp.exp(m2 - m1)
    denom = 1.0 + e
    idx_ref[...] = jnp.where(lane == 0, i1, jnp.where(lane == 1, i2, 0))
    gate_ref[...] = jnp.where(lane == 0, 1.0 / denom, jnp.where(lane == 1, e / denom, 0.0))


def router_top2(xp, w_router):
    m, half = xp.shape
    d, n_experts = w_router.shape
    w = jnp.zeros((d, ROUTER_LANES), BF16).at[:, :n_experts].set(w_router.astype(BF16))
    bm = _tile(m, 1024)
    out_spec = pl.BlockSpec((bm, ROUTER_LANES), lambda i: (i, 0))
    return pl.pallas_call(
        functools.partial(_router_kernel, n_experts=n_experts),
        grid=(m // bm,),
        in_specs=[pl.BlockSpec((bm, half), lambda i: (i, 0)),
                  pl.BlockSpec((d, ROUTER_LANES), lambda i: (0, 0))],
        out_specs=[out_spec, out_spec],
        out_shape=[jax.ShapeDtypeStruct((m, ROUTER_LANES), jnp.int32),
                   jax.ShapeDtypeStruct((m, ROUTER_LANES), F32)],
        compiler_params=_params(),
        name="router_top2",
    )(xp, w)


def moe_dispatch(idx, n_experts, bm):
    m = idx.shape[0]
    flat_e = idx[:, :TOP_K].reshape(-1)
    onehot = (flat_e[:, None] == jnp.arange(n_experts, dtype=jnp.int32)[None, :]).astype(jnp.int32)
    csum = jnp.cumsum(onehot, axis=0)
    rank = jnp.take_along_axis(csum, flat_e[:, None], axis=1)[:, 0] - 1
    tiles_e = (csum[-1] + bm - 1) // bm
    tile_end = jnp.cumsum(tiles_e)
    tile_start = tile_end - tiles_e
    pos = tile_start[flat_e] * bm + rank
    n_tiles = (m * TOP_K) // bm + n_experts
    src = jnp.zeros((n_tiles * bm,), jnp.int32).at[pos].set(
        jnp.arange(m * TOP_K, dtype=jnp.int32) // TOP_K)
    tile_ids = jnp.arange(n_tiles, dtype=jnp.int32)
    total = tile_end[-1]
    expert_of = lambda t: jnp.sum((tile_end[None, :] <= t[:, None]).astype(jnp.int32), axis=1)
    tile_expert = jnp.minimum(expert_of(tile_ids), expert_of((total - 1)[None])[0])
    tile_valid = (tile_ids < total).astype(jnp.int32)
    return src, pos, tile_expert, tile_valid


def _rope_tables(n_tokens, reps):
    rows = n_tokens // GRID_W
    row = jnp.repeat(jnp.arange(rows, dtype=F32), GRID_W)
    col = jnp.tile(jnp.arange(GRID_W, dtype=F32), rows)
    half = ROPE_DIM // 2
    inv = ROPE_THETA ** (-jnp.arange(0, half, 2, dtype=F32) / half)
    ang_r = row[:, None] * inv
    ang_c = col[:, None] * inv
    cos = jnp.concatenate([jnp.cos(ang_r)] * 2 + [jnp.cos(ang_c)] * 2, axis=1)
    sin = jnp.concatenate([-jnp.sin(ang_r), jnp.sin(ang_r), -jnp.sin(ang_c), jnp.sin(ang_c)], axis=1)
    return jnp.tile(cos, (1, reps)), jnp.tile(sin, (1, reps))


def _swap_perm():
    q = ROPE_DIM // 4
    return jnp.concatenate([jnp.arange(q, 2 * q), jnp.arange(0, q),
                            jnp.arange(3 * q, 4 * q), jnp.arange(2 * q, 3 * q)])


def _qkv_kernel(h_ref, w_ref, gq_ref, gkv_ref, cos_ref, sin_ref, qa_ref, ckv_ref, kr_ref,
                *, q_rank, kv_rank, n_prompt_tiles):
    z = _bf16_dot(h_ref[...], w_ref[...])

    def rms(v, g):
        return (v * lax.rsqrt(jnp.mean(v * v, axis=-1, keepdims=True) + EPS)) * g

    qa_ref[...] = rms(z[:, :q_rank], gq_ref[...]).astype(qa_ref.dtype)
    ckv_ref[...] = rms(z[:, q_rank:q_rank + kv_rank], gkv_ref[...])
    kr = z[:, q_rank + kv_rank:q_rank + kv_rank + ROPE_DIM]
    kr_sw = z[:, q_rank + kv_rank + ROPE_DIM:]
    is_prompt = pl.program_id(0) < n_prompt_tiles

    @pl.when(is_prompt)
    def _():
        kr_ref[...] = kr

    @pl.when(jnp.logical_not(is_prompt))
    def _():
        kr_ref[...] = kr * cos_ref[...] + kr_sw * sin_ref[...]


def qkv_compress(h, w_cat, g_q, g_kv, cos_k, sin_k, *, mp, ts):
    m, d = h.shape
    q_rank, kv_rank = g_q.shape[0], g_kv.shape[0]
    n = w_cat.shape[1]
    bm = _tile(math.gcd(mp, ts), 512)
    npt, tps = mp // bm, ts // bm
    tab_spec = pl.BlockSpec((bm, ROPE_DIM), lambda i: (jnp.maximum(i - npt, 0) % tps, 0))
    return pl.pallas_call(
        functools.partial(_qkv_kernel, q_rank=q_rank, kv_rank=kv_rank, n_prompt_tiles=npt),
        grid=(m // bm,),
        in_specs=[
            pl.BlockSpec((bm, d), lambda i: (i, 0)),
            pl.BlockSpec((d, n), lambda i: (0, 0)),
            pl.BlockSpec((1, q_rank), lambda i: (0, 0)),
            pl.BlockSpec((1, kv_rank), lambda i: (0, 0)),
            tab_spec, tab_spec,
        ],
        out_specs=[
            pl.BlockSpec((bm, q_rank), lambda i: (i, 0)),
            pl.BlockSpec((bm, kv_rank), lambda i: (i, 0)),
            pl.BlockSpec((bm, ROPE_DIM), lambda i: (i, 0)),
        ],
        out_shape=[
            jax.ShapeDtypeStruct((m, q_rank), BF16),
            jax.ShapeDtypeStruct((m, kv_rank), F32),
            jax.ShapeDtypeStruct((m, ROPE_DIM), F32),
        ],
        compiler_params=_params(),
        name="qkv_compress",
    )(h, w_cat, g_q.reshape(1, -1), g_kv.reshape(1, -1), cos_k, sin_k)


def _attn_kernel(*refs, use_rope, scale):
    if use_rope:
        qn_ref, qr_ref, qs_ref, cos_ref, sin_ref, kv_ref, kr_ref, o_ref = refs
        qr = qr_ref[...] * cos_ref[...] + qs_ref[...] * sin_ref[...]
    else:
        qn_ref, qr_ref, kv_ref, kr_ref, o_ref = refs
        qr = qr_ref[...]
    qr = qr.astype(BF16)
    kr = kr_ref[...]
    contract_last = (((1,), (1,)), ((), ()))
    hw = NOPE_DIM + V_DIM
    for hh in range(HEADS_PER_STEP):
        qn = qn_ref[:, hh * NOPE_DIM:(hh + 1) * NOPE_DIM]
        kn = kv_ref[:, hh * hw:hh * hw + NOPE_DIM]
        v = kv_ref[:, hh * hw + NOPE_DIM:(hh + 1) * hw]
        s = lax.dot_general(qn, kn, contract_last, preferred_element_type=F32)
        s = s + lax.dot_general(qr[:, hh * ROPE_DIM:(hh + 1) * ROPE_DIM], kr, contract_last,
                                preferred_element_type=F32)
        s = s * scale
        p = jnp.exp(s - jnp.max(s, axis=-1, keepdims=True))
        l = jnp.sum(p, axis=-1, keepdims=True)
        o = _bf16_dot(p.astype(BF16), v) / l
        o_ref[:, hh * V_DIM:(hh + 1) * V_DIM] = o.astype(o_ref.dtype)


def attention(qn, qr, kv, kr, *, row0, nseq, t, s_len, kv_row0, rope_tables):
    n_hp = N_HEADS // HEADS_PER_STEP
    tq = _tile(t, 256)
    tb = t // tq
    wn, wr, wv = HEADS_PER_STEP * NOPE_DIM, HEADS_PER_STEP * ROPE_DIM, HEADS_PER_STEP * V_DIM
    wkv = HEADS_PER_STEP * (NOPE_DIM + V_DIM)
    use_rope = rope_tables is not None
    qrow = lambda b, h, i: row0 // tq + b * tb + i
    in_specs = [pl.BlockSpec((tq, wn), lambda b, h, i: (qrow(b, h, i), h)),
                pl.BlockSpec((tq, wr), lambda b, h, i: (qrow(b, h, i), h))]
    args = [qn, qr]
    if use_rope:
        in_specs += [pl.BlockSpec((tq, wr), lambda b, h, i: (qrow(b, h, i), n_hp + h)),
                     pl.BlockSpec((tq, wr), lambda b, h, i: (i, 0)),
                     pl.BlockSpec((tq, wr), lambda b, h, i: (i, 0))]
        args += [qr, *rope_tables]
    in_specs += [pl.BlockSpec((s_len, wkv), lambda b, h, i: (kv_row0 // s_len + b, h)),
                 pl.BlockSpec((None, s_len, ROPE_DIM), lambda b, h, i: (b, 0, 0))]
    args += [kv, kr]
    return pl.pallas_call(
        functools.partial(_attn_kernel, use_rope=use_rope,
                          scale=float((NOPE_DIM + ROPE_DIM) ** -0.5)),
        grid=(nseq, n_hp, tb),
        in_specs=in_specs,
        out_specs=pl.BlockSpec((tq, wv), lambda b, h, i: (b * tb + i, h)),
        out_shape=jax.ShapeDtypeStruct((nseq * t, N_HEADS * V_DIM), BF16),
        compiler_params=_params(),
        name="attention_rope" if use_rope else "attention",
    )(*args)


def _rms_kernel(x_ref, g_ref, o_ref):
    x = x_ref[...]
    o_ref[...] = (x * lax.rsqrt(jnp.mean(x * x, axis=-1, keepdims=True) + EPS)) * g_ref[...]


def rms_rows(x, g, *, row0, rows):
    d = x.shape[1]
    bm = _tile(math.gcd(row0, rows) if row0 else rows, 256)
    return pl.pallas_call(
        _rms_kernel,
        grid=(rows // bm,),
        in_specs=[pl.BlockSpec((bm, d), lambda i: (row0 // bm + i, 0)),
                  pl.BlockSpec((1, d), lambda i: (0, 0))],
        out_specs=pl.BlockSpec((bm, d), lambda i: (i, 0)),
        out_shape=jax.ShapeDtypeStruct((rows, d), F32),
        compiler_params=_params(),
        name="final_norm",
    )(x, g.reshape(1, d))


def _ffn_tile_rows(m):
    return _tile(m, 1024)


def _fourier_layer(x, mod, g_mix, g_ffn, w_out, wg, wu, wd, *, mp, tp, ts):
    seq = dict(mp=mp, ts=ts)
    h = norm_mod(x, g_mix, mod, 0, 1, packed=False, **seq)
    f = fourier_mix(h, mp=mp, tp=tp, ts=ts)
    x = matmul_residual(f, w_out, x, mod, 2, name="fnet_out", **seq)
    h = norm_mod(x, g_ffn, mod, 3, 4, packed=False, **seq)
    bm = _ffn_tile_rows(x.shape[0])
    n_tiles = x.shape[0] // bm
    y = grouped_swiglu(h, wg[None], wu[None], wd[None], jnp.zeros((n_tiles,), jnp.int32),
                       jnp.ones((n_tiles,), jnp.int32), bm=bm)
    return gated_residual(x, mod, 5, y, **seq)


def _attention_layer(x, mod, g_mix, g_ffn, cache_ckv, cache_krope, w_q_a, g_q_a, w_q_b, w_kv_a, g_kv_a,
                     w_kv_b, w_o, w_router, wg, wu, wd, *, mp, tp, ts):
    seq = dict(mp=mp, ts=ts)
    m = x.shape[0]
    nb_p, nb_s = mp // tp, (m - mp) // ts
    past = cache_ckv.shape[1]
    q_rank, kv_rank = w_q_a.shape[1], g_kv_a.shape[0]
    swap = _swap_perm()

    h = norm_mod(x, g_mix, mod, 0, 1, packed=False, **seq)
    w_cat = jnp.concatenate([w_q_a, w_kv_a, w_kv_a[:, kv_rank:][:, swap]], axis=1).astype(BF16)
    cos_k, sin_k = _rope_tables(ts, 1)
    qa, ckv, kr = qkv_compress(h, w_cat, g_q_a, g_kv_a, cos_k, sin_k, **seq)

    wq = w_q_b.reshape(q_rank, N_HEADS, NOPE_DIM + ROPE_DIM)
    wq_nope = wq[:, :, :NOPE_DIM].reshape(q_rank, -1).astype(BF16)
    wq_rope = wq[:, :, NOPE_DIM:]
    wq_rope = jnp.concatenate([wq_rope.reshape(q_rank, -1), wq_rope[:, :, swap].reshape(q_rank, -1)],
                              axis=1).astype(BF16)
    qn = matmul(qa, wq_nope, BF16, bm=1024, bn=2048, name="q_nope")
    qr = matmul(qa, wq_rope, F32, bm=512, bn=2048, name="q_rope")

    s_len = past + ts
    ckv_s = jnp.concatenate([cache_ckv, ckv[mp:].reshape(nb_s, ts, kv_rank)], axis=1)
    ckv_all = jnp.concatenate([ckv_s.reshape(nb_s * s_len, kv_rank), ckv[:mp]], axis=0).astype(BF16)
    kv = matmul(ckv_all, w_kv_b.astype(BF16), BF16, bm=512, bn=4096, name="kv_up")
    kr_s = jnp.concatenate([cache_krope, kr[mp:].reshape(nb_s, ts, ROPE_DIM)], axis=1).astype(BF16)
    kr_p = kr[:mp].reshape(nb_p, tp, ROPE_DIM).astype(BF16)

    o_p = attention(qn, qr, kv, kr_p, row0=0, nseq=nb_p, t=tp, s_len=tp, kv_row0=nb_s * s_len,
                    rope_tables=None)
    o_s = attention(qn, qr, kv, kr_s, row0=mp, nseq=nb_s, t=ts, s_len=s_len, kv_row0=0,
                    rope_tables=_rope_tables(ts, HEADS_PER_STEP))
    o = jnp.concatenate([o_p, o_s], axis=0)
    x = matmul_residual(o, w_o, x, mod, 2, name="attn_out", **seq)

    hp = norm_mod(x, g_ffn, mod, 3, 4, packed=True, **seq)
    idx, gates = router_top2(hp, w_router)
    bm = _ffn_tile_rows(m)
    src, pos, tile_expert, tile_valid = moe_dispatch(idx, w_router.shape[1], bm)
    y = grouped_swiglu(gather_unpack(hp, src), wg, wu, wd, tile_expert, tile_valid, bm=bm)
    x = gather_combine(x, mod, 5, y, pos, gates, **seq)
    return x, ckv[:mp], kr[:mp]


def kernel(x_prompt, x_sample, cache_ckv, cache_krope, c, c_ctx, w_mod, b_mod, g_mix, g_ffn, w_fnet_out, w_ffn_gate, w_ffn_up, w_ffn_down, w_q_a, g_q_a, w_q_b, w_kv_a, g_kv_a, w_kv_b, w_o, w_router, w_exp_gate, w_exp_up, w_exp_down, g_final):
    bp, tp, d = x_prompt.shape
    bs, ts, _ = x_sample.shape
    mp, ms = bp * tp, bs * ts
    depth = w_mod.shape[0]
    assert 1 + bs <= COND_ROWS
    seqs = dict(mp=mp, tp=tp, ts=ts)

    x = jnp.concatenate([x_prompt.reshape(mp, d), x_sample.reshape(ms, d)], axis=0)
    cond = jnp.zeros((COND_ROWS, d), F32).at[0].set(c_ctx).at[1:1 + bs].set(c)

    new_ckv, new_krope = [], []
    for i in range(depth):
        j = i // 2
        mod = adaln(cond, w_mod, b_mod, i)
        if i % 2 == 0:
            x = _fourier_layer(x, mod, g_mix[i], g_ffn[i], w_fnet_out[j], w_ffn_gate[j], w_ffn_up[j],
                               w_ffn_down[j], **seqs)
        else:
            x, ckv_p, kr_p = _attention_layer(
                x, mod, g_mix[i], g_ffn[i], cache_ckv[:, j], cache_krope[:, j], w_q_a[j], g_q_a[j],
                w_q_b[j], w_kv_a[j], g_kv_a[j], w_kv_b[j], w_o[j], w_router[j], w_exp_gate[j],
                w_exp_up[j], w_exp_down[j], **seqs)
            new_ckv.append(ckv_p.reshape(bp, tp, -1))
            new_krope.append(kr_p.reshape(bp, tp, -1))

    y_prompt = rms_rows(x, g_final, row0=0, rows=mp).reshape(bp, tp, d)
    y_sample = rms_rows(x, g_final, row0=mp, rows=ms).reshape(bs, ts, d)
    return (y_prompt, y_sample, jnp.stack(new_ckv, axis=1), jnp.stack(new_krope, axis=1))
```

```python
import functools
import math

import jax
import jax.numpy as jnp
from jax import lax
from jax.experimental import pallas as pl
from jax.experimental.pallas import tpu as pltpu

GRID_W = 64
FNET_GROUPS = 8
N_HEADS = 32
NOPE_DIM = 128
ROPE_DIM = 64
V_DIM = 128
ROPE_THETA = 10000.0
TOP_K = 2
N_MOD = 6
EPS = 1e-6
COND_ROWS = 8
ROUTER_LANES = 128
V7X_VMEM_LIMIT_BYTES = 60 * 1024 * 1024

BF16 = jnp.bfloat16
F32 = jnp.float32


def _params():
    return pltpu.CompilerParams(vmem_limit_bytes=V7X_VMEM_LIMIT_BYTES)


def _tile(n, pref):
    t = min(n, pref)
    while n % t:
        t -= 1
    return t


def _bf16_dot(a, b):
    return jnp.dot(a, b, preferred_element_type=F32)


def _to_bf16(v):
    return v if v.dtype == BF16 else v.astype(BF16)


def _seg_of_tile(i, bm, mp, ts):
    r = i * bm
    return jnp.where(r < mp, 0, 1 + (r - mp) // ts)


def _adaln_kernel(c_ref, w_ref, b_ref, o_ref):
    c = c_ref[...]
    s = c * (1.0 / (1.0 + jnp.exp(-c)))
    o_ref[...] = _bf16_dot(s.astype(BF16), w_ref[...].astype(BF16)) + b_ref[...]


def adaln(cond, w_mod, b_mod, layer):
    r, d = cond.shape
    n = w_mod.shape[-1]
    bn = _tile(n, 1024)
    out = pl.pallas_call(
        _adaln_kernel,
        grid=(n // bn,),
        in_specs=[
            pl.BlockSpec((r, d), lambda j: (0, 0)),
            pl.BlockSpec((None, d, bn), lambda j: (layer, 0, j)),
            pl.BlockSpec((None, 1, bn), lambda j: (layer, 0, j)),
        ],
        out_specs=pl.BlockSpec((r, bn), lambda j: (0, j)),
        out_shape=jax.ShapeDtypeStruct((r, n), F32),
        compiler_params=_params(),
        name="adaln",
    )(cond, w_mod, b_mod.reshape(b_mod.shape[0], 1, n))
    return out.reshape(r * N_MOD, 1, d)


def _pack_bf16_pair(lo, hi):
    lo_bits = lax.bitcast_convert_type(lo.astype(BF16).astype(F32), jnp.uint32)
    hi_bits = lax.bitcast_convert_type(hi.astype(BF16).astype(F32), jnp.uint32)
    return (lo_bits >> 16) | hi_bits


def _unpack_bf16_pair(w):
    lo = lax.bitcast_convert_type(w << 16, F32).astype(BF16)
    hi = lax.bitcast_convert_type(w & jnp.uint32(0xFFFF0000), F32).astype(BF16)
    return lo, hi


def _row_operand(x, bm, bw, npt, col, buffers=None):
    mode = dict(pipeline_mode=pl.Buffered(buffers)) if buffers else {}
    if isinstance(x, tuple):
        return ([pl.BlockSpec((bm, bw), lambda i, *r: (jnp.minimum(i, npt - 1), col(*r)), **mode),
                 pl.BlockSpec((bm, bw), lambda i, *r: (jnp.maximum(i - npt, 0), col(*r)), **mode)],
                list(x))
    return [pl.BlockSpec((bm, bw), lambda i, *r: (i, col(*r)), **mode)], [x]


def _for_row_half(npt, n_variants, body):
    if n_variants == 1:
        body(0)
        return
    i = pl.program_id(0)
    pl.when(i < npt)(lambda: body(0))
    pl.when(i >= npt)(lambda: body(-1))


def _norm_mod_kernel(*refs, n_x, npt, packed):
    x_refs = refs[:n_x]
    g_ref, sh_ref, sc_ref, o_ref = refs[n_x:]

    def body(k):
        x = x_refs[k][...]
        y = x * lax.rsqrt(jnp.mean(x * x, axis=-1, keepdims=True) + EPS)
        y = y * g_ref[...]
        h = y * (1.0 + sc_ref[...]) + sh_ref[...]
        if packed:
            half = h.shape[-1] // 2
            o_ref[...] = _pack_bf16_pair(h[:, :half], h[:, half:])
        else:
            o_ref[...] = h.astype(o_ref.dtype)

    _for_row_half(npt, n_x, body)


def norm_mod(x, g, mod, k_shift, k_scale, *, mp, ts, packed):
    d = g.shape[0]
    m = mp + (x[1].shape[0] if isinstance(x, tuple) else x.shape[0] - mp)
    bm = _tile(math.gcd(mp, ts), 256)
    seg = functools.partial(_seg_of_tile, bm=bm, mp=mp, ts=ts)
    out_d, out_dtype = (d // 2, jnp.uint32) if packed else (d, BF16)
    x_specs, x_args = _row_operand(x, bm, d, mp // bm, lambda: 0)
    return pl.pallas_call(
        functools.partial(_norm_mod_kernel, n_x=len(x_args), npt=mp // bm, packed=packed),
        grid=(m // bm,),
        in_specs=x_specs + [
            pl.BlockSpec((1, d), lambda i: (0, 0)),
            pl.BlockSpec((None, 1, d), lambda i: (seg(i) * N_MOD + k_shift, 0, 0)),
            pl.BlockSpec((None, 1, d), lambda i: (seg(i) * N_MOD + k_scale, 0, 0)),
        ],
        out_specs=pl.BlockSpec((bm, out_d), lambda i: (i, 0)),
        out_shape=jax.ShapeDtypeStruct((m, out_d), out_dtype),
        compiler_params=_params(),
        name="norm_mod",
    )(*x_args, g.reshape(1, d), mod, mod)


def _mm_kernel(a_ref, w_ref, o_ref):
    o_ref[...] = _bf16_dot(a_ref[...], _to_bf16(w_ref[...])).astype(o_ref.dtype)


def matmul(a, w, out_dtype, *, bm, bn, name):
    m, k = a.shape
    n = w.shape[1]
    bm, bn = _tile(m, bm), _tile(n, bn)
    return pl.pallas_call(
        _mm_kernel,
        grid=(m // bm, n // bn),
        in_specs=[
            pl.BlockSpec((bm, k), lambda i, j: (i, 0)),
            pl.BlockSpec((k, bn), lambda i, j: (0, j)),
        ],
        out_specs=pl.BlockSpec((bm, bn), lambda i, j: (i, j)),
        out_shape=jax.ShapeDtypeStruct((m, n), out_dtype),
        compiler_params=_params(),
        name=name,
    )(a, w)


def _mm_res_kernel(*refs, n_a, n_res, npt):
    a_refs, w_ref = refs[:n_a], refs[n_a]
    res_refs = refs[n_a + 1:n_a + 1 + n_res]
    gate_ref, o_ref = refs[-2:]

    def body(k):
        acc = _bf16_dot(a_refs[k][...], _to_bf16(w_ref[...]))
        o_ref[...] = res_refs[k][...] + gate_ref[...] * acc

    _for_row_half(npt, max(n_a, n_res), body)


def matmul_residual(a, w, res, mod, k_gate, *, m, mp, ts, name):
    k, n = w.shape
    bm = _tile(math.gcd(mp, ts), 1024)
    bn = _tile(n, 512)
    npt = mp // bm
    seg = functools.partial(_seg_of_tile, bm=bm, mp=mp, ts=ts)
    a_specs, a_args = _row_operand(a, bm, k, npt, lambda j: 0, buffers=1)
    res_specs, res_args = _row_operand(res, bm, bn, npt, lambda j: j)
    return pl.pallas_call(
        functools.partial(_mm_res_kernel, n_a=len(a_args), n_res=len(res_args), npt=npt),
        grid=(m // bm, n // bn),
        in_specs=a_specs + [pl.BlockSpec((k, bn), lambda i, j: (0, j))] + res_specs + [
            pl.BlockSpec((None, 1, bn), lambda i, j: (seg(i) * N_MOD + k_gate, 0, j)),
        ],
        out_specs=pl.BlockSpec((bm, bn), lambda i, j: (i, j)),
        out_shape=jax.ShapeDtypeStruct((m, n), F32),
        compiler_params=_params(),
        name=name,
    )(*a_args, w, *res_args, mod)


def _dft_matrices(n):
    r = _tile(n, 64)
    k = jnp.arange(n, dtype=jnp.int32)

    def cos_sin(rows):
        ang = ((rows[:, None] * k[None, :]) % n).astype(F32) * (2.0 * math.pi / n)
        return jnp.cos(ang), jnp.sin(ang)

    ca, sa = cos_sin(jnp.arange(n // r, dtype=jnp.int32) * r)
    cb, sb = cos_sin(jnp.arange(r, dtype=jnp.int32))
    ca, sa, cb, sb = ca[:, None, :], sa[:, None, :], cb[None, :, :], sb[None, :, :]
    s = n ** -0.5
    return ((ca * cb - sa * sb) * s).reshape(n, n), ((sa * cb + ca * sb) * s).reshape(n, n)


def _dft_channel_kernel(h_ref, w_ref, o_ref):
    c = h_ref.shape[-1]
    y = _bf16_dot(h_ref[...], w_ref[...])
    o_ref[0] = y[:, :c].astype(o_ref.dtype)
    o_ref[1] = y[:, c:].astype(o_ref.dtype)


def dft_channels(h, w_cs, *, row0, nseq, t):
    d = h.shape[1]
    c = d // FNET_GROUPS
    bm = _tile(t, 1024)
    tb = t // bm
    return pl.pallas_call(
        _dft_channel_kernel,
        grid=(nseq * tb, FNET_GROUPS),
        in_specs=[
            pl.BlockSpec((bm, c), lambda i, g: (row0 // bm + i, g)),
            pl.BlockSpec((c, 2 * c), lambda i, g: (0, 0)),
        ],
        out_specs=pl.BlockSpec((None, 2, bm, c), lambda i, g: (i // tb, 0, i % tb, g)),
        out_shape=jax.ShapeDtypeStruct((nseq, 2, t, d), BF16),
        compiler_params=_params(),
        name="dft_channels",
    )(h, w_cs)


def _bmm_kernel(a_ref, b_ref, o_ref):
    o_ref[...] = _bf16_dot(a_ref[...], b_ref[...]).astype(o_ref.dtype)


def dft_positions(a_cs, y):
    nseq, k, d = y.shape
    t = a_cs.shape[0]
    bm, bn = _tile(t, 512), _tile(d, 512)
    tb = t // bm
    return pl.pallas_call(
        _bmm_kernel,
        grid=(nseq, tb, d // bn),
        in_specs=[
            pl.BlockSpec((bm, k), lambda b, i, j: (i, 0)),
            pl.BlockSpec((None, k, bn), lambda b, i, j: (b, 0, j)),
        ],
        out_specs=pl.BlockSpec((bm, bn), lambda b, i, j: (b * tb + i, j)),
        out_shape=jax.ShapeDtypeStruct((nseq * t, d), BF16),
        compiler_params=_params(),
        name="dft_positions",
    )(a_cs, y)


def fourier_mix(h, *, mp, tp, ts):
    m, d = h.shape
    c = d // FNET_GROUPS
    cc, sc = _dft_matrices(c)
    w_cs = jnp.concatenate([cc, sc], axis=1).astype(BF16)
    outs = []
    for row0, rows, t in ((0, mp, tp), (mp, m - mp, ts)):
        nseq = rows // t
        ct, st = _dft_matrices(t)
        a_cs = jnp.concatenate([ct, -st], axis=1).astype(BF16)
        y = dft_channels(h, w_cs, row0=row0, nseq=nseq, t=t)
        outs.append(dft_positions(a_cs, y.reshape(nseq, 2 * t, d)))
    return tuple(outs)


def _ffn_kernel(te_ref, tv_ref, x_ref, wg_ref, wu_ref, wd_ref, o_ref, *, col_chunk):
    del te_ref
    i, f = pl.program_id(0), pl.program_id(1)

    @pl.when(f == 0)
    def _():
        o_ref[...] = jnp.zeros_like(o_ref)

    @pl.when(tv_ref[i] > 0)
    def _():
        x = x_ref[...]
        g = _bf16_dot(x, wg_ref[...].astype(BF16))
        u = _bf16_dot(x, wu_ref[...].astype(BF16))
        h = ((g * (1.0 / (1.0 + jnp.exp(-g)))) * u).astype(BF16)
        for c0 in range(0, o_ref.shape[-1], col_chunk):
            cols = slice(c0, c0 + col_chunk)
            o_ref[:, cols] += _bf16_dot(h, wd_ref[:, cols].astype(BF16))


def grouped_swiglu(x, wg, wu, wd, tile_expert, tile_valid, *, bm):
    p, d = x.shape
    ff = wg.shape[-1]
    bf = _tile(ff, 256)
    nf = ff // bf
    n_tiles = p // bm

    def f_eff(i, f, tv):
        return jnp.where(tv[i] > 0, f, nf - 1)

    grid_spec = pltpu.PrefetchScalarGridSpec(
        num_scalar_prefetch=2,
        grid=(n_tiles, nf),
        in_specs=[
            pl.BlockSpec((bm, d), lambda i, f, te, tv: (i, 0), pipeline_mode=pl.Buffered(1)),
            pl.BlockSpec((None, d, bf), lambda i, f, te, tv: (te[i], 0, f_eff(i, f, tv))),
            pl.BlockSpec((None, d, bf), lambda i, f, te, tv: (te[i], 0, f_eff(i, f, tv))),
            pl.BlockSpec((None, bf, d), lambda i, f, te, tv: (te[i], f_eff(i, f, tv), 0)),
        ],
        out_specs=pl.BlockSpec((bm, d), lambda i, f, te, tv: (i, 0), pipeline_mode=pl.Buffered(1)),
    )
    return pl.pallas_call(
        functools.partial(_ffn_kernel, col_chunk=_tile(d, 512)),
        grid_spec=grid_spec,
        out_shape=jax.ShapeDtypeStruct((p, d), F32),
        compiler_params=_params(),
        name="grouped_swiglu",
    )(tile_expert, tile_valid, x, wg, wu, wd)


def _row_copy(src_ref, dst_ref, sem, src_row, dst_row):
    return pltpu.make_async_copy(src_ref.at[pl.ds(src_row, 1)], dst_ref.at[pl.ds(dst_row, 1)], sem)


def _start_row_gather(idx_ref, idx0, idx_stride, src_ref, dst_ref, sem, rows):
    def body(r, carry):
        _row_copy(src_ref, dst_ref, sem, idx_ref[idx0 + r * idx_stride], r).start()
        return carry
    lax.fori_loop(0, rows, body, 0)


def _wait_row_gather(src_ref, dst_ref, sem, rows):
    def body(r, carry):
        _row_copy(src_ref, dst_ref, sem, 0, r).wait()
        return carry
    lax.fori_loop(0, rows, body, 0)


def _double_buffered_step(start, wait):
    i, n = pl.program_id(0), pl.num_programs(0)
    slot = i % 2
    pl.when(i == 0)(lambda: start(i, slot))
    pl.when(i + 1 < n)(lambda: start(i + 1, 1 - slot))
    wait(slot)
    return slot


def _gather_unpack_kernel(idx_ref, src_ref, o_ref, buf_ref, sem, *, rows):
    def start(step, slot):
        _start_row_gather(idx_ref, step * rows, 1, src_ref, buf_ref.at[slot], sem.at[slot], rows)

    def wait(slot):
        _wait_row_gather(src_ref, buf_ref.at[slot], sem.at[slot], rows)

    slot = _double_buffered_step(start, wait)
    half = buf_ref.shape[-1]
    lo, hi = _unpack_bf16_pair(buf_ref[slot])
    o_ref[:, :half] = lo
    o_ref[:, half:] = hi


def gather_unpack(src, idx):
    n = idx.shape[0]
    half = src.shape[1]
    rows = _tile(n, 512)
    grid_spec = pltpu.PrefetchScalarGridSpec(
        num_scalar_prefetch=1,
        grid=(n // rows,),
        in_specs=[pl.BlockSpec(memory_space=pl.ANY)],
        out_specs=pl.BlockSpec((rows, 2 * half), lambda i, idx: (i, 0)),
        scratch_shapes=[pltpu.VMEM((2, rows, half), src.dtype), pltpu.SemaphoreType.DMA((2,))],
    )
    return pl.pallas_call(
        functools.partial(_gather_unpack_kernel, rows=rows),
        grid_spec=grid_spec,
        out_shape=jax.ShapeDtypeStruct((n, 2 * half), BF16),
        compiler_params=_params(),
        name="gather_unpack",
    )(idx, src)


def _gather_combine_kernel(pos_ref, x_ref, gate_ref, w_ref, y_ref, o_ref, buf_ref, sem, *, rows):
    def start(step, slot):
        for k in range(TOP_K):
            _start_row_gather(pos_ref, step * rows * TOP_K + k, TOP_K, y_ref, buf_ref.at[slot, k],
                              sem.at[slot, k], rows)

    def wait(slot):
        for k in range(TOP_K):
            _wait_row_gather(y_ref, buf_ref.at[slot, k], sem.at[slot, k], rows)

    slot = _double_buffered_step(start, wait)
    w = w_ref[...]
    ff = w[:, 0:1] * buf_ref[slot, 0]
    for k in range(1, TOP_K):
        ff = ff + w[:, k:k + 1] * buf_ref[slot, k]
    o_ref[...] = x_ref[...] + gate_ref[...] * ff


def gather_combine(x, mod, k_gate, y, pos, weights, *, mp, ts):
    m, d = x.shape
    rows = _tile(math.gcd(mp, ts), 256)
    seg = functools.partial(_seg_of_tile, bm=rows, mp=mp, ts=ts)
    row_spec = pl.BlockSpec((rows, d), lambda i, pos: (i, 0))
    grid_spec = pltpu.PrefetchScalarGridSpec(
        num_scalar_prefetch=1,
        grid=(m // rows,),
        in_specs=[
            row_spec,
            pl.BlockSpec((None, 1, d), lambda i, pos: (seg(i) * N_MOD + k_gate, 0, 0)),
            pl.BlockSpec((rows, weights.shape[1]), lambda i, pos: (i, 0)),
            pl.BlockSpec(memory_space=pl.ANY),
        ],
        out_specs=row_spec,
        scratch_shapes=[pltpu.VMEM((2, TOP_K, rows, d), F32), pltpu.SemaphoreType.DMA((2, TOP_K))],
    )
    return pl.pallas_call(
        functools.partial(_gather_combine_kernel, rows=rows),
        grid_spec=grid_spec,
        out_shape=jax.ShapeDtypeStruct((m, d), F32),
        compiler_params=_params(),
        name="gather_combine",
    )(pos, x, mod, weights, y)


def _residual_kernel(x_ref, gate_ref, y_ref, o_ref):
    o_ref[...] = x_ref[...] + gate_ref[...] * y_ref[...]


def gated_residual(x, mod, k_gate, y, *, mp, ts):
    m, d = x.shape
    bm = _tile(math.gcd(mp, ts), 256)
    seg = functools.partial(_seg_of_tile, bm=bm, mp=mp, ts=ts)
    row_spec = pl.BlockSpec((bm, d), lambda i: (i, 0))
    return pl.pallas_call(
        _residual_kernel,
        grid=(m // bm,),
        in_specs=[row_spec, pl.BlockSpec((None, 1, d), lambda i: (seg(i) * N_MOD + k_gate, 0, 0)),
                  row_spec],
        out_specs=row_spec,
        out_shape=jax.ShapeDtypeStruct((m, d), F32),
        compiler_params=_params(),
        name="gated_residual",
    )(x, mod, y)


def _router_kernel(x_ref, w_ref, idx_ref, gate_ref, *, n_experts):
    half = x_ref.shape[-1]
    lo, hi = _unpack_bf16_pair(x_ref[...])
    logits = _bf16_dot(lo, w_ref[:half, :]) + _bf16_dot(hi, w_ref[half:, :])
    lane = lax.broadcasted_iota(jnp.int32, logits.shape, 1)
    neg = jnp.float32(-jnp.inf)
    lg = jnp.where(lane < n_experts, logits, neg)
    m1 = jnp.max(lg, axis=-1, keepdims=True)
    i1 = jnp.min(jnp.where(lg == m1, lane, ROUTER_LANES), axis=-1, keepdims=True)
    lg2 = jnp.where(lane == i1, neg, lg)
    m2 = jnp.max(lg2, axis=-1, keepdims=True)
    i2 = jnp.min(jnp.where(lg2 == m2, lane, ROUTER_LANES), axis=-1, keepdims=True)
    e = jnp.exp(m2 - m1)
    denom = 1.0 + e
    idx_ref[...] = jnp.where(lane == 0, i1, jnp.where(lane == 1, i2, 0))
    gate_ref[...] = jnp.where(lane == 0, 1.0 / denom, jnp.where(lane == 1, e / denom, 0.0))


def router_top2(xp, w_router):
    m, half = xp.shape
    d, n_experts = w_router.shape
    w = jnp.zeros((d, ROUTER_LANES), BF16).at[:, :n_experts].set(w_router.astype(BF16))
    bm = _tile(m, 1024)
    out_spec = pl.BlockSpec((bm, ROUTER_LANES), lambda i: (i, 0))
    return pl.pallas_call(
        functools.partial(_router_kernel, n_experts=n_experts),
        grid=(m // bm,),
        in_specs=[pl.BlockSpec((bm, half), lambda i: (i, 0)),
                  pl.BlockSpec((d, ROUTER_LANES), lambda i: (0, 0))],
        out_specs=[out_spec, out_spec],
        out_shape=[jax.ShapeDtypeStruct((m, ROUTER_LANES), jnp.int32),
                   jax.ShapeDtypeStruct((m, ROUTER_LANES), F32)],
        compiler_params=_params(),
        name="router_top2",
    )(xp, w)


def moe_dispatch(idx, n_experts, bm):
    m = idx.shape[0]
    flat_e = idx[:, :TOP_K].reshape(-1)
    onehot = (flat_e[:, None] == jnp.arange(n_experts, dtype=jnp.int32)[None, :]).astype(jnp.int32)
    csum = jnp.cumsum(onehot, axis=0)
    rank = jnp.take_along_axis(csum, flat_e[:, None], axis=1)[:, 0] - 1
    tiles_e = (csum[-1] + bm - 1) // bm
    tile_end = jnp.cumsum(tiles_e)
    tile_start = tile_end - tiles_e
    pos = tile_start[flat_e] * bm + rank
    n_tiles = (m * TOP_K) // bm + n_experts
    src = jnp.zeros((n_tiles * bm,), jnp.int32).at[pos].set(
        jnp.arange(m * TOP_K, dtype=jnp.int32) // TOP_K)
    tile_ids = jnp.arange(n_tiles, dtype=jnp.int32)
    total = tile_end[-1]
    expert_of = lambda t: jnp.sum((tile_end[None, :] <= t[:, None]).astype(jnp.int32), axis=1)
    tile_expert = jnp.minimum(expert_of(tile_ids), expert_of((total - 1)[None])[0])
    tile_valid = (tile_ids < total).astype(jnp.int32)
    return src, pos, tile_expert, tile_valid


def _rope_mix_tables(n_tokens, bm, lead):
    rows = n_tokens // GRID_W
    row = jnp.repeat(jnp.arange(rows, dtype=F32), GRID_W)
    col = jnp.tile(jnp.arange(GRID_W, dtype=F32), rows)
    half = ROPE_DIM // 2
    inv = ROPE_THETA ** (-jnp.arange(0, half, 2, dtype=F32) / half)
    ang_r = row[:, None] * inv
    ang_c = col[:, None] * inv
    cos = jnp.concatenate([jnp.cos(ang_r)] * 2 + [jnp.cos(ang_c)] * 2, axis=1)
    sin = jnp.concatenate([-jnp.sin(ang_r), jnp.sin(ang_r), -jnp.sin(ang_c), jnp.sin(ang_c)], axis=1)
    cos = jnp.concatenate([cos, jnp.ones((bm, ROPE_DIM), F32)], axis=0)
    sin = jnp.concatenate([sin, jnp.zeros((bm, ROPE_DIM), F32)], axis=0)
    n = n_tokens + bm
    mul = jnp.concatenate([jnp.ones((n, lead), F32), cos, jnp.zeros((n, ROPE_DIM), F32)], axis=1)
    add = jnp.concatenate([jnp.zeros((n, lead), F32), sin, jnp.zeros((n, ROPE_DIM), F32)], axis=1)
    return mul, add


def _rope_mix(z, mul, add):
    return z * mul + pltpu.roll(z, z.shape[-1] - ROPE_DIM, 1) * add


def _rope_table_spec(bm, width, npt, tps):
    return pl.BlockSpec((bm, width), lambda i, *_: (jnp.where(i < npt, tps, (i - npt) % tps), 0))


def _swap_perm():
    q = ROPE_DIM // 4
    return jnp.concatenate([jnp.arange(q, 2 * q), jnp.arange(0, q),
                            jnp.arange(3 * q, 4 * q), jnp.arange(2 * q, 3 * q)])


def _qkv_kernel(h_ref, w_ref, gq_ref, gkv_ref, mul_ref, add_ref, qa_ref, ckv_ref, kr_ref,
                *, q_rank, kv_rank):
    z = _bf16_dot(h_ref[...], w_ref[...])

    def rms(v, g):
        return (v * lax.rsqrt(jnp.mean(v * v, axis=-1, keepdims=True) + EPS)) * g

    qa_ref[...] = rms(z[:, :q_rank], gq_ref[...]).astype(qa_ref.dtype)
    ckv_ref[...] = rms(z[:, q_rank:q_rank + kv_rank], gkv_ref[...])
    kr_ref[...] = _rope_mix(z[:, q_rank + kv_rank:], mul_ref[...], add_ref[...])


def qkv_compress(h, w_cat, g_q, g_kv, *, mp, ts):
    m, d = h.shape
    q_rank, kv_rank = g_q.shape[0], g_kv.shape[0]
    n = w_cat.shape[1]
    bm = _tile(math.gcd(mp, ts), 512)
    mul, add = _rope_mix_tables(ts, bm, 0)
    tab_spec = _rope_table_spec(bm, 2 * ROPE_DIM, mp // bm, ts // bm)
    return pl.pallas_call(
        functools.partial(_qkv_kernel, q_rank=q_rank, kv_rank=kv_rank),
        grid=(m // bm,),
        in_specs=[
            pl.BlockSpec((bm, d), lambda i: (i, 0)),
            pl.BlockSpec((d, n), lambda i: (0, 0)),
            pl.BlockSpec((1, q_rank), lambda i: (0, 0)),
            pl.BlockSpec((1, kv_rank), lambda i: (0, 0)),
            tab_spec, tab_spec,
        ],
        out_specs=[
            pl.BlockSpec((bm, q_rank), lambda i: (i, 0)),
            pl.BlockSpec((bm, kv_rank), lambda i: (i, 0)),
            pl.BlockSpec((bm, 2 * ROPE_DIM), lambda i: (i, 0)),
        ],
        out_shape=[
            jax.ShapeDtypeStruct((m, q_rank), BF16),
            jax.ShapeDtypeStruct((m, kv_rank), F32),
            jax.ShapeDtypeStruct((m, 2 * ROPE_DIM), F32),
        ],
        compiler_params=_params(),
        name="qkv_compress",
    )(h, w_cat, g_q.reshape(1, -1), g_kv.reshape(1, -1), mul, add)


def _q_up_kernel(a_ref, w_ref, mul_ref, add_ref, o_ref):
    acc = _bf16_dot(a_ref[...], w_ref[...])
    qw = mul_ref.shape[-1]
    for c0 in range(0, acc.shape[-1], qw):
        cols = slice(c0, c0 + qw)
        o_ref[:, cols] = _rope_mix(acc[:, cols], mul_ref[...], add_ref[...]).astype(o_ref.dtype)


def q_up(qa, wq_cat, *, mp, ts):
    m, k = qa.shape
    n = wq_cat.shape[1]
    qw = NOPE_DIM + 2 * ROPE_DIM
    bm = _tile(math.gcd(mp, ts), 512)
    bn = qw * _tile(n // qw, 8)
    mul, add = _rope_mix_tables(ts, bm, NOPE_DIM)
    tab_spec = _rope_table_spec(bm, qw, mp // bm, ts // bm)
    return pl.pallas_call(
        _q_up_kernel,
        grid=(m // bm, n // bn),
        in_specs=[pl.BlockSpec((bm, k), lambda i, j: (i, 0)),
                  pl.BlockSpec((k, bn), lambda i, j: (0, j)),
                  tab_spec, tab_spec],
        out_specs=pl.BlockSpec((bm, bn), lambda i, j: (i, j)),
        out_shape=jax.ShapeDtypeStruct((m, n), BF16),
        compiler_params=_params(),
        name="q_up",
    )(qa, wq_cat, mul, add)


def _k_up_kernel(a_ref, w_ref, kr_ref, o_ref):
    acc = _bf16_dot(a_ref[...], w_ref[...])
    kr = kr_ref[...]
    qw = NOPE_DIM + kr.shape[-1]
    for hh in range(acc.shape[-1] // NOPE_DIM):
        o_ref[:, hh * qw:hh * qw + NOPE_DIM] = acc[:, hh * NOPE_DIM:(hh + 1) * NOPE_DIM].astype(o_ref.dtype)
        o_ref[:, hh * qw + NOPE_DIM:(hh + 1) * qw] = kr


def k_up(ckv, w_kn, kr):
    m, k = ckv.shape
    qw = NOPE_DIM + 2 * ROPE_DIM
    bm = _tile(m, 512)
    hb = _tile(N_HEADS, 16)
    return pl.pallas_call(
        _k_up_kernel,
        grid=(m // bm, N_HEADS // hb),
        in_specs=[pl.BlockSpec((bm, k), lambda i, j: (i, 0)),
                  pl.BlockSpec((k, hb * NOPE_DIM), lambda i, j: (0, j)),
                  pl.BlockSpec((bm, 2 * ROPE_DIM), lambda i, j: (i, 0))],
        out_specs=pl.BlockSpec((bm, hb * qw), lambda i, j: (i, j)),
        out_shape=jax.ShapeDtypeStruct((m, N_HEADS * qw), BF16),
        compiler_params=_params(),
        name="k_up",
    )(ckv, w_kn, kr)


def _attn_kernel(q_ref, k_ref, v_ref, o_ref, *, heads, exp2_scale):
    qw = q_ref.shape[-1] // heads
    contract_last = (((1,), (1,)), ((), ()))
    for hh in range(heads):
        q = q_ref[:, hh * qw:(hh + 1) * qw]
        k = k_ref[:, hh * qw:(hh + 1) * qw]
        v = v_ref[:, hh * V_DIM:(hh + 1) * V_DIM]
        s = lax.dot_general(q, k, contract_last, preferred_element_type=F32)
        p = jnp.exp2((s - jnp.max(s, axis=-1, keepdims=True)) * exp2_scale)
        l = jnp.sum(p, axis=-1, keepdims=True)
        o = _bf16_dot(p.astype(BF16), v) / l
        o_ref[:, hh * V_DIM:(hh + 1) * V_DIM] = o.astype(o_ref.dtype)


def attention(q, k, v, *, row0, nseq, t, s_len, kv_row0, heads, tq):
    assert kv_row0 % s_len == 0 and row0 % tq == 0
    qw = NOPE_DIM + 2 * ROPE_DIM
    tb = t // tq
    scale = (NOPE_DIM + ROPE_DIM) ** -0.5
    kv_map = lambda b, h, i: (kv_row0 // s_len + b, h)
    return pl.pallas_call(
        functools.partial(_attn_kernel, heads=heads, exp2_scale=float(scale * math.log2(math.e))),
        grid=(nseq, N_HEADS // heads, tb),
        in_specs=[pl.BlockSpec((tq, heads * qw), lambda b, h, i: (row0 // tq + b * tb + i, h)),
                  pl.BlockSpec((s_len, heads * qw), kv_map),
                  pl.BlockSpec((s_len, heads * V_DIM), kv_map)],
        out_specs=pl.BlockSpec((tq, heads * V_DIM), lambda b, h, i: (b * tb + i, h)),
        out_shape=jax.ShapeDtypeStruct((nseq * t, N_HEADS * V_DIM), BF16),
        compiler_params=_params(),
        name="attention",
    )(q, k, v)


def _rms_kernel(x_ref, g_ref, o_ref):
    x = x_ref[...]
    o_ref[...] = (x * lax.rsqrt(jnp.mean(x * x, axis=-1, keepdims=True) + EPS)) * g_ref[...]


def rms_rows(x, g, *, row0, rows):
    d = x.shape[1]
    bm = _tile(math.gcd(row0, rows) if row0 else rows, 256)
    return pl.pallas_call(
        _rms_kernel,
        grid=(rows // bm,),
        in_specs=[pl.BlockSpec((bm, d), lambda i: (row0 // bm + i, 0)),
                  pl.BlockSpec((1, d), lambda i: (0, 0))],
        out_specs=pl.BlockSpec((bm, d), lambda i: (i, 0)),
        out_shape=jax.ShapeDtypeStruct((rows, d), F32),
        compiler_params=_params(),
        name="final_norm",
    )(x, g.reshape(1, d))


def _ffn_tile_rows(m):
    return _tile(m, 1024)


def _fourier_layer(x, mod, g_mix, g_ffn, w_out, wg, wu, wd, *, m, mp, tp, ts):
    seq = dict(mp=mp, ts=ts)
    h = norm_mod(x, g_mix, mod, 0, 1, packed=False, **seq)
    f = fourier_mix(h, mp=mp, tp=tp, ts=ts)
    x = matmul_residual(f, w_out, x, mod, 2, m=m, name="fnet_out", **seq)
    h = norm_mod(x, g_ffn, mod, 3, 4, packed=False, **seq)
    bm = _ffn_tile_rows(m)
    n_tiles = m // bm
    y = grouped_swiglu(h, wg[None], wu[None], wd[None], jnp.zeros((n_tiles,), jnp.int32),
                       jnp.ones((n_tiles,), jnp.int32), bm=bm)
    return gated_residual(x, mod, 5, y, **seq)


def _attention_layer(x, mod, g_mix, g_ffn, cache_ckv, cache_krope, w_q_a, g_q_a, w_q_b, w_kv_a, g_kv_a,
                     w_kv_b, w_o, w_router, wg, wu, wd, *, m, mp, tp, ts):
    seq = dict(mp=mp, ts=ts)
    nb_p, nb_s = mp // tp, (m - mp) // ts
    past = cache_ckv.shape[1]
    q_rank, kv_rank = w_q_a.shape[1], g_kv_a.shape[0]
    swap = _swap_perm()

    h = norm_mod(x, g_mix, mod, 0, 1, packed=False, **seq)
    w_cat = jnp.concatenate([w_q_a, w_kv_a, w_kv_a[:, kv_rank:][:, swap]], axis=1).astype(BF16)
    qa, ckv, kr = qkv_compress(h, w_cat, g_q_a, g_kv_a, **seq)

    wq = w_q_b.reshape(q_rank, N_HEADS, NOPE_DIM + ROPE_DIM)
    wq_cat = jnp.concatenate([wq, wq[:, :, NOPE_DIM:][:, :, swap]], axis=2).reshape(q_rank, -1)
    q = q_up(qa, wq_cat.astype(BF16), **seq)

    s_len = past + ts
    ckv_s = jnp.concatenate([cache_ckv, ckv[mp:].reshape(nb_s, ts, kv_rank)], axis=1)
    ckv_all = jnp.concatenate([ckv_s.reshape(nb_s * s_len, kv_rank), ckv[:mp]], axis=0).astype(BF16)
    cache_kr = jnp.pad(cache_krope, ((0, 0), (0, 0), (0, ROPE_DIM)))
    kr_s = jnp.concatenate([cache_kr, kr[mp:].reshape(nb_s, ts, 2 * ROPE_DIM)], axis=1)
    kr_all = jnp.concatenate([kr_s.reshape(nb_s * s_len, 2 * ROPE_DIM), kr[:mp]], axis=0).astype(BF16)
    wkv = w_kv_b.reshape(kv_rank, N_HEADS, NOPE_DIM + V_DIM)
    k = k_up(ckv_all, wkv[:, :, :NOPE_DIM].reshape(kv_rank, -1).astype(BF16), kr_all)
    v = matmul(ckv_all, wkv[:, :, NOPE_DIM:].reshape(kv_rank, -1).astype(BF16), BF16,
               bm=512, bn=4096, name="v_up")

    o_p = attention(q, k, v, row0=0, nseq=nb_p, t=tp, s_len=tp, kv_row0=nb_s * s_len,
                    heads=_tile(N_HEADS, 16), tq=_tile(tp, 256))
    o_s = attention(q, k, v, row0=mp, nseq=nb_s, t=ts, s_len=s_len, kv_row0=0,
                    heads=_tile(N_HEADS, 4), tq=_tile(ts, 256))
    x = matmul_residual((o_p, o_s), w_o, x, mod, 2, m=m, name="attn_out", **seq)

    hp = norm_mod(x, g_ffn, mod, 3, 4, packed=True, **seq)
    idx, gates = router_top2(hp, w_router)
    bm = _ffn_tile_rows(m)
    src, pos, tile_expert, tile_valid = moe_dispatch(idx, w_router.shape[1], bm)
    y = grouped_swiglu(gather_unpack(hp, src), wg, wu, wd, tile_expert, tile_valid, bm=bm)
    x = gather_combine(x, mod, 5, y, pos, gates, **seq)
    return x, ckv[:mp], kr[:mp, :ROPE_DIM]


def kernel(x_prompt, x_sample, cache_ckv, cache_krope, c, c_ctx, w_mod, b_mod, g_mix, g_ffn, w_fnet_out, w_ffn_gate, w_ffn_up, w_ffn_down, w_q_a, g_q_a, w_q_b, w_kv_a, g_kv_a, w_kv_b, w_o, w_router, w_exp_gate, w_exp_up, w_exp_down, g_final):
    bp, tp, d = x_prompt.shape
    bs, ts, _ = x_sample.shape
    mp, ms = bp * tp, bs * ts
    depth = w_mod.shape[0]
    assert 1 + bs <= COND_ROWS
    seqs = dict(m=mp + ms, mp=mp, tp=tp, ts=ts)

    x = (x_prompt.reshape(mp, d), x_sample.reshape(ms, d))
    cond = jnp.zeros((COND_ROWS, d), F32).at[0].set(c_ctx).at[1:1 + bs].set(c)

    new_ckv, new_krope = [], []
    for i in range(depth):
        j = i // 2
        mod = adaln(cond, w_mod, b_mod, i)
        if i % 2 == 0:
            x = _fourier_layer(x, mod, g_mix[i], g_ffn[i], w_fnet_out[j], w_ffn_gate[j], w_ffn_up[j],
                               w_ffn_down[j], **seqs)
        else:
            x, ckv_p, kr_p = _attention_layer(
                x, mod, g_mix[i], g_ffn[i], cache_ckv[:, j], cache_krope[:, j], w_q_a[j], g_q_a[j],
                w_q_b[j], w_kv_a[j], g_kv_a[j], w_kv_b[j], w_o[j], w_router[j], w_exp_gate[j],
                w_exp_up[j], w_exp_down[j], **seqs)
            new_ckv.append(ckv_p.reshape(bp, tp, -1))
            new_krope.append(kr_p.reshape(bp, tp, -1))

    y_prompt = rms_rows(x, g_final, row0=0, rows=mp).reshape(bp, tp, d)
    y_sample = rms_rows(x, g_final, row0=mp, rows=ms).reshape(bs, ts, d)
    return (y_prompt, y_sample, jnp.stack(new_ckv, axis=1), jnp.stack(new_krope, axis=1))
```

```python
import functools
import math

import jax
import jax.numpy as jnp
from jax import lax
from jax.experimental import pallas as pl
from jax.experimental.pallas import tpu as pltpu

GRID_W = 64
FNET_GROUPS = 8
N_HEADS = 32
NOPE_DIM = 128
ROPE_DIM = 64
V_DIM = 128
ROPE_THETA = 10000.0
TOP_K = 2
N_MOD = 6
EPS = 1e-6
COND_ROWS = 8
ROUTER_LANES = 128
N_DMA_PRIORITIES = 2
V7X_VMEM_LIMIT_BYTES = 60 * 1024 * 1024

BF16 = jnp.bfloat16
F32 = jnp.float32


def _params():
    return pltpu.CompilerParams(vmem_limit_bytes=V7X_VMEM_LIMIT_BYTES)


def _tile(n, pref):
    t = min(n, pref)
    while n % t:
        t -= 1
    return t


def _bf16_dot(a, b):
    return jnp.dot(a, b, preferred_element_type=F32)


def _to_bf16(v):
    return v if v.dtype == BF16 else v.astype(BF16)


def _seg_of_tile(i, bm, mp, ts):
    r = i * bm
    return jnp.where(r < mp, 0, 1 + (r - mp) // ts)


def _adaln_kernel(c_ref, w_ref, b_ref, o_ref):
    c = c_ref[...]
    s = c * (1.0 / (1.0 + jnp.exp(-c)))
    o_ref[...] = _bf16_dot(s.astype(BF16), w_ref[...].astype(BF16)) + b_ref[...]


def adaln(cond, w_mod, b_mod, layer):
    r, d = cond.shape
    n = w_mod.shape[-1]
    bn = _tile(n, 1024)
    out = pl.pallas_call(
        _adaln_kernel,
        grid=(n // bn,),
        in_specs=[
            pl.BlockSpec((r, d), lambda j: (0, 0)),
            pl.BlockSpec((None, d, bn), lambda j: (layer, 0, j)),
            pl.BlockSpec((None, 1, bn), lambda j: (layer, 0, j)),
        ],
        out_specs=pl.BlockSpec((r, bn), lambda j: (0, j)),
        out_shape=jax.ShapeDtypeStruct((r, n), F32),
        compiler_params=_params(),
        name="adaln",
    )(cond, w_mod, b_mod.reshape(b_mod.shape[0], 1, n))
    return out.reshape(r * N_MOD, 1, d)


def _pack_bf16_pair(lo, hi):
    lo_bits = lax.bitcast_convert_type(lo.astype(BF16).astype(F32), jnp.uint32)
    hi_bits = lax.bitcast_convert_type(hi.astype(BF16).astype(F32), jnp.uint32)
    return (lo_bits >> 16) | hi_bits


def _unpack_bf16_pair(w):
    lo = lax.bitcast_convert_type(w << 16, F32).astype(BF16)
    hi = lax.bitcast_convert_type(w & jnp.uint32(0xFFFF0000), F32).astype(BF16)
    return lo, hi


def _row_operand(x, bm, bw, npt, col, buffers=None):
    mode = dict(pipeline_mode=pl.Buffered(buffers)) if buffers else {}
    if isinstance(x, tuple):
        return ([pl.BlockSpec((bm, bw), lambda i, *r: (jnp.minimum(i, npt - 1), col(*r)), **mode),
                 pl.BlockSpec((bm, bw), lambda i, *r: (jnp.maximum(i - npt, 0), col(*r)), **mode)],
                list(x))
    return [pl.BlockSpec((bm, bw), lambda i, *r: (i, col(*r)), **mode)], [x]


def _for_row_half(npt, n_variants, body):
    if n_variants == 1:
        body(0)
        return
    i = pl.program_id(0)
    pl.when(i < npt)(lambda: body(0))
    pl.when(i >= npt)(lambda: body(-1))


def _norm_mod_kernel(*refs, n_x, npt, packed):
    x_refs = refs[:n_x]
    g_ref, sh_ref, sc_ref, o_ref = refs[n_x:]

    def body(k):
        x = x_refs[k][...]
        y = x * lax.rsqrt(jnp.mean(x * x, axis=-1, keepdims=True) + EPS)
        y = y * g_ref[...]
        h = y * (1.0 + sc_ref[...]) + sh_ref[...]
        if packed:
            half = h.shape[-1] // 2
            o_ref[...] = _pack_bf16_pair(h[:, :half], h[:, half:])
        else:
            o_ref[...] = h.astype(o_ref.dtype)

    _for_row_half(npt, n_x, body)


def norm_mod(x, g, mod, k_shift, k_scale, *, mp, ts, packed):
    d = g.shape[0]
    m = mp + (x[1].shape[0] if isinstance(x, tuple) else x.shape[0] - mp)
    bm = _tile(math.gcd(mp, ts), 256)
    seg = functools.partial(_seg_of_tile, bm=bm, mp=mp, ts=ts)
    out_d, out_dtype = (d // 2, jnp.uint32) if packed else (d, BF16)
    x_specs, x_args = _row_operand(x, bm, d, mp // bm, lambda: 0)
    return pl.pallas_call(
        functools.partial(_norm_mod_kernel, n_x=len(x_args), npt=mp // bm, packed=packed),
        grid=(m // bm,),
        in_specs=x_specs + [
            pl.BlockSpec((1, d), lambda i: (0, 0)),
            pl.BlockSpec((None, 1, d), lambda i: (seg(i) * N_MOD + k_shift, 0, 0)),
            pl.BlockSpec((None, 1, d), lambda i: (seg(i) * N_MOD + k_scale, 0, 0)),
        ],
        out_specs=pl.BlockSpec((bm, out_d), lambda i: (i, 0)),
        out_shape=jax.ShapeDtypeStruct((m, out_d), out_dtype),
        compiler_params=_params(),
        name="norm_mod",
    )(*x_args, g.reshape(1, d), mod, mod)


def _mm_kernel(a_ref, w_ref, o_ref):
    o_ref[...] = _bf16_dot(a_ref[...], _to_bf16(w_ref[...])).astype(o_ref.dtype)


def matmul(a, w, out_dtype, *, bm, bn, name):
    m, k = a.shape
    n = w.shape[1]
    bm, bn = _tile(m, bm), _tile(n, bn)
    return pl.pallas_call(
        _mm_kernel,
        grid=(m // bm, n // bn),
        in_specs=[
            pl.BlockSpec((bm, k), lambda i, j: (i, 0)),
            pl.BlockSpec((k, bn), lambda i, j: (0, j)),
        ],
        out_specs=pl.BlockSpec((bm, bn), lambda i, j: (i, j)),
        out_shape=jax.ShapeDtypeStruct((m, n), out_dtype),
        compiler_params=_params(),
        name=name,
    )(a, w)


def _mm_res_kernel(*refs, n_a, n_res, npt):
    a_refs, w_ref = refs[:n_a], refs[n_a]
    res_refs = refs[n_a + 1:n_a + 1 + n_res]
    gate_ref, o_ref = refs[-2:]

    def body(k):
        acc = _bf16_dot(a_refs[k][...], _to_bf16(w_ref[...]))
        o_ref[...] = res_refs[k][...] + gate_ref[...] * acc

    _for_row_half(npt, max(n_a, n_res), body)


def matmul_residual(a, w, res, mod, k_gate, *, m, mp, ts, name):
    k, n = w.shape
    bm = _tile(math.gcd(mp, ts), 1024)
    bn = _tile(n, 512)
    npt = mp // bm
    seg = functools.partial(_seg_of_tile, bm=bm, mp=mp, ts=ts)
    a_specs, a_args = _row_operand(a, bm, k, npt, lambda j: 0, buffers=1)
    res_specs, res_args = _row_operand(res, bm, bn, npt, lambda j: j)
    return pl.pallas_call(
        functools.partial(_mm_res_kernel, n_a=len(a_args), n_res=len(res_args), npt=npt),
        grid=(m // bm, n // bn),
        in_specs=a_specs + [pl.BlockSpec((k, bn), lambda i, j: (0, j))] + res_specs + [
            pl.BlockSpec((None, 1, bn), lambda i, j: (seg(i) * N_MOD + k_gate, 0, j)),
        ],
        out_specs=pl.BlockSpec((bm, bn), lambda i, j: (i, j)),
        out_shape=jax.ShapeDtypeStruct((m, n), F32),
        compiler_params=_params(),
        name=name,
    )(*a_args, w, *res_args, mod)


def _dft_matrices(n):
    r = _tile(n, 64)
    k = jnp.arange(n, dtype=jnp.int32)

    def cos_sin(rows):
        ang = ((rows[:, None] * k[None, :]) % n).astype(F32) * (2.0 * math.pi / n)
        return jnp.cos(ang), jnp.sin(ang)

    ca, sa = cos_sin(jnp.arange(n // r, dtype=jnp.int32) * r)
    cb, sb = cos_sin(jnp.arange(r, dtype=jnp.int32))
    ca, sa, cb, sb = ca[:, None, :], sa[:, None, :], cb[None, :, :], sb[None, :, :]
    s = n ** -0.5
    return ((ca * cb - sa * sb) * s).reshape(n, n), ((sa * cb + ca * sb) * s).reshape(n, n)


def _dft_channel_kernel(h_ref, w_ref, o_ref):
    c = h_ref.shape[-1]
    y = _bf16_dot(h_ref[...], w_ref[...])
    o_ref[0] = y[:, :c].astype(o_ref.dtype)
    o_ref[1] = y[:, c:].astype(o_ref.dtype)


def dft_channels(h, w_cs, *, row0, nseq, t):
    d = h.shape[1]
    c = d // FNET_GROUPS
    bm = _tile(t, 1024)
    tb = t // bm
    return pl.pallas_call(
        _dft_channel_kernel,
        grid=(nseq * tb, FNET_GROUPS),
        in_specs=[
            pl.BlockSpec((bm, c), lambda i, g: (row0 // bm + i, g)),
            pl.BlockSpec((c, 2 * c), lambda i, g: (0, 0)),
        ],
        out_specs=pl.BlockSpec((None, 2, bm, c), lambda i, g: (i // tb, 0, i % tb, g)),
        out_shape=jax.ShapeDtypeStruct((nseq, 2, t, d), BF16),
        compiler_params=_params(),
        name="dft_channels",
    )(h, w_cs)


def _bmm_kernel(a_ref, b_ref, o_ref):
    o_ref[...] = _bf16_dot(a_ref[...], b_ref[...]).astype(o_ref.dtype)


def dft_positions(a_cs, y):
    nseq, k, d = y.shape
    t = a_cs.shape[0]
    bm, bn = _tile(t, 512), _tile(d, 512)
    tb = t // bm
    return pl.pallas_call(
        _bmm_kernel,
        grid=(nseq, tb, d // bn),
        in_specs=[
            pl.BlockSpec((bm, k), lambda b, i, j: (i, 0)),
            pl.BlockSpec((None, k, bn), lambda b, i, j: (b, 0, j)),
        ],
        out_specs=pl.BlockSpec((bm, bn), lambda b, i, j: (b * tb + i, j)),
        out_shape=jax.ShapeDtypeStruct((nseq * t, d), BF16),
        compiler_params=_params(),
        name="dft_positions",
    )(a_cs, y)


def fourier_mix(h, *, mp, tp, ts):
    m, d = h.shape
    c = d // FNET_GROUPS
    cc, sc = _dft_matrices(c)
    w_cs = jnp.concatenate([cc, sc], axis=1).astype(BF16)
    outs = []
    for row0, rows, t in ((0, mp, tp), (mp, m - mp, ts)):
        nseq = rows // t
        ct, st = _dft_matrices(t)
        a_cs = jnp.concatenate([ct, -st], axis=1).astype(BF16)
        y = dft_channels(h, w_cs, row0=row0, nseq=nseq, t=t)
        outs.append(dft_positions(a_cs, y.reshape(nseq, 2 * t, d)))
    return tuple(outs)


def _ffn_kernel(te_ref, tv_ref, x_ref, wg_ref, wu_ref, wd_ref, o_ref, *, col_chunk):
    del te_ref
    i, f = pl.program_id(0), pl.program_id(1)

    @pl.when(f == 0)
    def _():
        o_ref[...] = jnp.zeros_like(o_ref)

    @pl.when(tv_ref[i] > 0)
    def _():
        x = x_ref[...]
        g = _bf16_dot(x, wg_ref[...].astype(BF16))
        u = _bf16_dot(x, wu_ref[...].astype(BF16))
        h = ((g * (1.0 / (1.0 + jnp.exp(-g)))) * u).astype(BF16)
        for c0 in range(0, o_ref.shape[-1], col_chunk):
            cols = slice(c0, c0 + col_chunk)
            o_ref[:, cols] += _bf16_dot(h, wd_ref[:, cols].astype(BF16))


def grouped_swiglu(x, wg, wu, wd, tile_expert, tile_valid, *, bm):
    p, d = x.shape
    ff = wg.shape[-1]
    bf = _tile(ff, 256)
    nf = ff // bf
    n_tiles = p // bm

    def f_eff(i, f, tv):
        return jnp.where(tv[i] > 0, f, nf - 1)

    grid_spec = pltpu.PrefetchScalarGridSpec(
        num_scalar_prefetch=2,
        grid=(n_tiles, nf),
        in_specs=[
            pl.BlockSpec((bm, d), lambda i, f, te, tv: (i, 0), pipeline_mode=pl.Buffered(1)),
            pl.BlockSpec((None, d, bf), lambda i, f, te, tv: (te[i], 0, f_eff(i, f, tv))),
            pl.BlockSpec((None, d, bf), lambda i, f, te, tv: (te[i], 0, f_eff(i, f, tv))),
            pl.BlockSpec((None, bf, d), lambda i, f, te, tv: (te[i], f_eff(i, f, tv), 0)),
        ],
        out_specs=pl.BlockSpec((bm, d), lambda i, f, te, tv: (i, 0), pipeline_mode=pl.Buffered(1)),
    )
    return pl.pallas_call(
        functools.partial(_ffn_kernel, col_chunk=_tile(d, 512)),
        grid_spec=grid_spec,
        out_shape=jax.ShapeDtypeStruct((p, d), F32),
        compiler_params=_params(),
        name="grouped_swiglu",
    )(tile_expert, tile_valid, x, wg, wu, wd)


def _row_copy(src_ref, dst_ref, sem, src_row, dst_row):
    return pltpu.make_async_copy(src_ref.at[pl.ds(src_row, 1)], dst_ref.at[pl.ds(dst_row, 1)], sem)


def _start_row_gather(idx_ref, idx0, idx_stride, src_ref, dst_ref, sem, rows):
    assert rows % N_DMA_PRIORITIES == 0

    def body(r2, carry):
        for p in range(N_DMA_PRIORITIES):
            r = r2 * N_DMA_PRIORITIES + p
            _row_copy(src_ref, dst_ref, sem, idx_ref[idx0 + r * idx_stride], r).start(priority=p)
        return carry
    lax.fori_loop(0, rows // N_DMA_PRIORITIES, body, 0)


def _wait_row_gather(src_ref, dst_ref, sem, rows):
    def body(r, carry):
        _row_copy(src_ref, dst_ref, sem, 0, r).wait()
        return carry
    lax.fori_loop(0, rows, body, 0)


def _double_buffered_step(start, wait):
    i, n = pl.program_id(0), pl.num_programs(0)
    slot = i % 2
    pl.when(i == 0)(lambda: start(i, slot))
    pl.when(i + 1 < n)(lambda: start(i + 1, 1 - slot))
    wait(slot)
    return slot


def _gather_unpack_kernel(idx_ref, src_ref, o_ref, buf_ref, sem, *, rows):
    def start(step, slot):
        _start_row_gather(idx_ref, step * rows, 1, src_ref, buf_ref.at[slot], sem.at[slot], rows)

    def wait(slot):
        _wait_row_gather(src_ref, buf_ref.at[slot], sem.at[slot], rows)

    slot = _double_buffered_step(start, wait)
    half = buf_ref.shape[-1]
    lo, hi = _unpack_bf16_pair(buf_ref[slot])
    o_ref[:, :half] = lo
    o_ref[:, half:] = hi


def gather_unpack(src, idx):
    n = idx.shape[0]
    half = src.shape[1]
    rows = _tile(n, 512)
    grid_spec = pltpu.PrefetchScalarGridSpec(
        num_scalar_prefetch=1,
        grid=(n // rows,),
        in_specs=[pl.BlockSpec(memory_space=pl.ANY)],
        out_specs=pl.BlockSpec((rows, 2 * half), lambda i, idx: (i, 0)),
        scratch_shapes=[pltpu.VMEM((2, rows, half), src.dtype), pltpu.SemaphoreType.DMA((2,))],
    )
    return pl.pallas_call(
        functools.partial(_gather_unpack_kernel, rows=rows),
        grid_spec=grid_spec,
        out_shape=jax.ShapeDtypeStruct((n, 2 * half), BF16),
        compiler_params=_params(),
        name="gather_unpack",
    )(idx, src)


def _gather_combine_kernel(pos_ref, x_ref, gate_ref, w_ref, y_ref, o_ref, buf_ref, sem, *, rows):
    def start(step, slot):
        for k in range(TOP_K):
            _start_row_gather(pos_ref, step * rows * TOP_K + k, TOP_K, y_ref, buf_ref.at[slot, k],
                              sem.at[slot, k], rows)

    def wait(slot):
        for k in range(TOP_K):
            _wait_row_gather(y_ref, buf_ref.at[slot, k], sem.at[slot, k], rows)

    slot = _double_buffered_step(start, wait)
    w = w_ref[...]
    ff = w[:, 0:1] * buf_ref[slot, 0]
    for k in range(1, TOP_K):
        ff = ff + w[:, k:k + 1] * buf_ref[slot, k]
    o_ref[...] = x_ref[...] + gate_ref[...] * ff


def gather_combine(x, mod, k_gate, y, pos, weights, *, mp, ts):
    m, d = x.shape
    rows = _tile(math.gcd(mp, ts), 256)
    seg = functools.partial(_seg_of_tile, bm=rows, mp=mp, ts=ts)
    row_spec = pl.BlockSpec((rows, d), lambda i, pos: (i, 0))
    grid_spec = pltpu.PrefetchScalarGridSpec(
        num_scalar_prefetch=1,
        grid=(m // rows,),
        in_specs=[
            row_spec,
            pl.BlockSpec((None, 1, d), lambda i, pos: (seg(i) * N_MOD + k_gate, 0, 0)),
            pl.BlockSpec((rows, weights.shape[1]), lambda i, pos: (i, 0)),
            pl.BlockSpec(memory_space=pl.ANY),
        ],
        out_specs=row_spec,
        scratch_shapes=[pltpu.VMEM((2, TOP_K, rows, d), F32), pltpu.SemaphoreType.DMA((2, TOP_K))],
    )
    return pl.pallas_call(
        functools.partial(_gather_combine_kernel, rows=rows),
        grid_spec=grid_spec,
        out_shape=jax.ShapeDtypeStruct((m, d), F32),
        compiler_params=_params(),
        name="gather_combine",
    )(pos, x, mod, weights, y)


def _residual_kernel(x_ref, gate_ref, y_ref, o_ref):
    o_ref[...] = x_ref[...] + gate_ref[...] * y_ref[...]


def gated_residual(x, mod, k_gate, y, *, mp, ts):
    m, d = x.shape
    bm = _tile(math.gcd(mp, ts), 256)
    seg = functools.partial(_seg_of_tile, bm=bm, mp=mp, ts=ts)
    row_spec = pl.BlockSpec((bm, d), lambda i: (i, 0))
    return pl.pallas_call(
        _residual_kernel,
        grid=(m // bm,),
        in_specs=[row_spec, pl.BlockSpec((None, 1, d), lambda i: (seg(i) * N_MOD + k_gate, 0, 0)),
                  row_spec],
        out_specs=row_spec,
        out_shape=jax.ShapeDtypeStruct((m, d), F32),
        compiler_params=_params(),
        name="gated_residual",
    )(x, mod, y)


def _router_kernel(x_ref, w_ref, idx_ref, gate_ref, *, n_experts):
    half = x_ref.shape[-1]
    lo, hi = _unpack_bf16_pair(x_ref[...])
    logits = _bf16_dot(lo, w_ref[:half, :]) + _bf16_dot(hi, w_ref[half:, :])
    lane = lax.broadcasted_iota(jnp.int32, logits.shape, 1)
    neg = jnp.float32(-jnp.inf)
    lg = jnp.where(lane < n_experts, logits, neg)
    m1 = jnp.max(lg, axis=-1, keepdims=True)
    i1 = jnp.min(jnp.where(lg == m1, lane, ROUTER_LANES), axis=-1, keepdims=True)
    lg2 = jnp.where(lane == i1, neg, lg)
    m2 = jnp.max(lg2, axis=-1, keepdims=True)
    i2 = jnp.min(jnp.where(lg2 == m2, lane, ROUTER_LANES), axis=-1, keepdims=True)
    e = jnp.exp(m2 - m1)
    denom = 1.0 + e
    idx_ref[...] = jnp.where(lane == 0, i1, jnp.where(lane == 1, i2, 0))
    gate_ref[...] = jnp.where(lane == 0, 1.0 / denom, jnp.where(lane == 1, e / denom, 0.0))


def router_top2(xp, w_router):
    m, half = xp.shape
    d, n_experts = w_router.shape
    w = jnp.zeros((d, ROUTER_LANES), BF16).at[:, :n_experts].set(w_router.astype(BF16))
    bm = _tile(m, 1024)
    out_spec = pl.BlockSpec((bm, ROUTER_LANES), lambda i: (i, 0))
    return pl.pallas_call(
        functools.partial(_router_kernel, n_experts=n_experts),
        grid=(m // bm,),
        in_specs=[pl.BlockSpec((bm, half), lambda i: (i, 0)),
                  pl.BlockSpec((d, ROUTER_LANES), lambda i: (0, 0))],
        out_specs=[out_spec, out_spec],
        out_shape=[jax.ShapeDtypeStruct((m, ROUTER_LANES), jnp.int32),
                   jax.ShapeDtypeStruct((m, ROUTER_LANES), F32)],
        compiler_params=_params(),
        name="router_top2",
    )(xp, w)


def moe_dispatch(idx, n_experts, bm):
    m = idx.shape[0]
    flat_e = idx[:, :TOP_K].reshape(-1)
    onehot = (flat_e[:, None] == jnp.arange(n_experts, dtype=jnp.int32)[None, :]).astype(jnp.int32)
    csum = jnp.cumsum(onehot, axis=0)
    rank = jnp.take_along_axis(csum, flat_e[:, None], axis=1)[:, 0] - 1
    tiles_e = (csum[-1] + bm - 1) // bm
    tile_end = jnp.cumsum(tiles_e)
    tile_start = tile_end - tiles_e
    pos = tile_start[flat_e] * bm + rank
    n_tiles = (m * TOP_K) // bm + n_experts
    src = jnp.zeros((n_tiles * bm,), jnp.int32).at[pos].set(
        jnp.arange(m * TOP_K, dtype=jnp.int32) // TOP_K)
    tile_ids = jnp.arange(n_tiles, dtype=jnp.int32)
    total = tile_end[-1]
    expert_of = lambda t: jnp.sum((tile_end[None, :] <= t[:, None]).astype(jnp.int32), axis=1)
    tile_expert = jnp.minimum(expert_of(tile_ids), expert_of((total - 1)[None])[0])
    tile_valid = (tile_ids < total).astype(jnp.int32)
    return src, pos, tile_expert, tile_valid


def _rope_mix_tables(n_tokens, bm, lead):
    rows = n_tokens // GRID_W
    row = jnp.repeat(jnp.arange(rows, dtype=F32), GRID_W)
    col = jnp.tile(jnp.arange(GRID_W, dtype=F32), rows)
    half = ROPE_DIM // 2
    inv = ROPE_THETA ** (-jnp.arange(0, half, 2, dtype=F32) / half)
    ang_r = row[:, None] * inv
    ang_c = col[:, None] * inv
    cos = jnp.concatenate([jnp.cos(ang_r)] * 2 + [jnp.cos(ang_c)] * 2, axis=1)
    sin = jnp.concatenate([-jnp.sin(ang_r), jnp.sin(ang_r), -jnp.sin(ang_c), jnp.sin(ang_c)], axis=1)
    cos = jnp.concatenate([cos, jnp.ones((bm, ROPE_DIM), F32)], axis=0)
    sin = jnp.concatenate([sin, jnp.zeros((bm, ROPE_DIM), F32)], axis=0)
    n = n_tokens + bm
    mul = jnp.concatenate([jnp.ones((n, lead), F32), cos, jnp.zeros((n, ROPE_DIM), F32)], axis=1)
    add = jnp.concatenate([jnp.zeros((n, lead), F32), sin, jnp.zeros((n, ROPE_DIM), F32)], axis=1)
    return mul, add


def _rope_mix(z, mul, add):
    return z * mul + pltpu.roll(z, z.shape[-1] - ROPE_DIM, 1) * add


def _rope_table_spec(bm, width, npt, tps):
    return pl.BlockSpec((bm, width), lambda i, *_: (jnp.where(i < npt, tps, (i - npt) % tps), 0))


def _swap_perm():
    q = ROPE_DIM // 4
    return jnp.concatenate([jnp.arange(q, 2 * q), jnp.arange(0, q),
                            jnp.arange(3 * q, 4 * q), jnp.arange(2 * q, 3 * q)])


def _qkv_kernel(h_ref, w_ref, gq_ref, gkv_ref, mul_ref, add_ref, qa_ref, ckv_ref, kr_ref,
                *, q_rank, kv_rank):
    z = _bf16_dot(h_ref[...], w_ref[...])

    def rms(v, g):
        return (v * lax.rsqrt(jnp.mean(v * v, axis=-1, keepdims=True) + EPS)) * g

    qa_ref[...] = rms(z[:, :q_rank], gq_ref[...]).astype(qa_ref.dtype)
    ckv_ref[...] = rms(z[:, q_rank:q_rank + kv_rank], gkv_ref[...])
    kr_ref[...] = _rope_mix(z[:, q_rank + kv_rank:], mul_ref[...], add_ref[...])


def qkv_compress(h, w_cat, g_q, g_kv, *, mp, ts):
    m, d = h.shape
    q_rank, kv_rank = g_q.shape[0], g_kv.shape[0]
    n = w_cat.shape[1]
    bm = _tile(math.gcd(mp, ts), 512)
    mul, add = _rope_mix_tables(ts, bm, 0)
    tab_spec = _rope_table_spec(bm, 2 * ROPE_DIM, mp // bm, ts // bm)
    return pl.pallas_call(
        functools.partial(_qkv_kernel, q_rank=q_rank, kv_rank=kv_rank),
        grid=(m // bm,),
        in_specs=[
            pl.BlockSpec((bm, d), lambda i: (i, 0)),
            pl.BlockSpec((d, n), lambda i: (0, 0)),
            pl.BlockSpec((1, q_rank), lambda i: (0, 0)),
            pl.BlockSpec((1, kv_rank), lambda i: (0, 0)),
            tab_spec, tab_spec,
        ],
        out_specs=[
            pl.BlockSpec((bm, q_rank), lambda i: (i, 0)),
            pl.BlockSpec((bm, kv_rank), lambda i: (i, 0)),
            pl.BlockSpec((bm, 2 * ROPE_DIM), lambda i: (i, 0)),
        ],
        out_shape=[
            jax.ShapeDtypeStruct((m, q_rank), BF16),
            jax.ShapeDtypeStruct((m, kv_rank), F32),
            jax.ShapeDtypeStruct((m, 2 * ROPE_DIM), F32),
        ],
        compiler_params=_params(),
        name="qkv_compress",
    )(h, w_cat, g_q.reshape(1, -1), g_kv.reshape(1, -1), mul, add)


def _q_up_kernel(a_ref, w_ref, mul_ref, add_ref, o_ref):
    acc = _bf16_dot(a_ref[...], w_ref[...])
    qw = mul_ref.shape[-1]
    for c0 in range(0, acc.shape[-1], qw):
        cols = slice(c0, c0 + qw)
        o_ref[:, cols] = _rope_mix(acc[:, cols], mul_ref[...], add_ref[...]).astype(o_ref.dtype)


def q_up(qa, wq_cat, *, mp, ts):
    m, k = qa.shape
    n = wq_cat.shape[1]
    qw = NOPE_DIM + 2 * ROPE_DIM
    bm = _tile(math.gcd(mp, ts), 512)
    bn = qw * _tile(n // qw, 8)
    mul, add = _rope_mix_tables(ts, bm, NOPE_DIM)
    tab_spec = _rope_table_spec(bm, qw, mp // bm, ts // bm)
    return pl.pallas_call(
        _q_up_kernel,
        grid=(m // bm, n // bn),
        in_specs=[pl.BlockSpec((bm, k), lambda i, j: (i, 0)),
                  pl.BlockSpec((k, bn), lambda i, j: (0, j)),
                  tab_spec, tab_spec],
        out_specs=pl.BlockSpec((bm, bn), lambda i, j: (i, j)),
        out_shape=jax.ShapeDtypeStruct((m, n), BF16),
        compiler_params=_params(),
        name="q_up",
    )(qa, wq_cat, mul, add)


def _head_up_kernel(a_ref, w_ref, fill_ref, o_ref, *, cw):
    acc = _bf16_dot(a_ref[...], w_ref[...])
    fill = fill_ref[...]
    ow = cw + fill.shape[-1]
    for hh in range(acc.shape[-1] // cw):
        o_ref[:, hh * ow:hh * ow + cw] = acc[:, hh * cw:(hh + 1) * cw].astype(o_ref.dtype)
        o_ref[:, hh * ow + cw:(hh + 1) * ow] = fill


def head_up(ckv, w, fill, *, name):
    m, k = ckv.shape
    cw = w.shape[1] // N_HEADS
    ow = cw + fill.shape[1]
    bm = _tile(m, 512)
    hb = _tile(N_HEADS, 16)
    return pl.pallas_call(
        functools.partial(_head_up_kernel, cw=cw),
        grid=(m // bm, N_HEADS // hb),
        in_specs=[pl.BlockSpec((bm, k), lambda i, j: (i, 0)),
                  pl.BlockSpec((k, hb * cw), lambda i, j: (0, j)),
                  pl.BlockSpec((bm, fill.shape[1]), lambda i, j: (i, 0))],
        out_specs=pl.BlockSpec((bm, hb * ow), lambda i, j: (i, j)),
        out_shape=jax.ShapeDtypeStruct((m, N_HEADS * ow), BF16),
        compiler_params=_params(),
        name=name,
    )(ckv, w, fill)


def _attn_kernel(q_ref, k_ref, v_ref, o_ref, *, heads, exp2_scale):
    qw = q_ref.shape[-1] // heads
    vw = v_ref.shape[-1] // heads
    contract_last = (((1,), (1,)), ((), ()))
    for hh in range(heads):
        q = q_ref[:, hh * qw:(hh + 1) * qw]
        k = k_ref[:, hh * qw:(hh + 1) * qw]
        v = v_ref[:, hh * vw:(hh + 1) * vw]
        s = lax.dot_general(q, k, contract_last, preferred_element_type=F32)
        p = jnp.exp2((s - jnp.max(s, axis=-1, keepdims=True)) * exp2_scale)
        l = jnp.sum(p, axis=-1, keepdims=True)
        o = _bf16_dot(p.astype(BF16), v) / l
        o_ref[:, hh * V_DIM:(hh + 1) * V_DIM] = o.astype(o_ref.dtype)


def attention(q, k, v, *, row0, nseq, t, s_len, kv_row0, heads, tq):
    assert kv_row0 % s_len == 0 and row0 % tq == 0
    qw = NOPE_DIM + 2 * ROPE_DIM
    vw = v.shape[1] // N_HEADS
    tb = t // tq
    scale = (NOPE_DIM + ROPE_DIM) ** -0.5
    kv_map = lambda b, h, i: (kv_row0 // s_len + b, h)
    return pl.pallas_call(
        functools.partial(_attn_kernel, heads=heads, exp2_scale=float(scale * math.log2(math.e))),
        grid=(nseq, N_HEADS // heads, tb),
        in_specs=[pl.BlockSpec((tq, heads * qw), lambda b, h, i: (row0 // tq + b * tb + i, h)),
                  pl.BlockSpec((s_len, heads * qw), kv_map),
                  pl.BlockSpec((s_len, heads * vw), kv_map)],
        out_specs=pl.BlockSpec((tq, heads * V_DIM), lambda b, h, i: (b * tb + i, h)),
        out_shape=jax.ShapeDtypeStruct((nseq * t, N_HEADS * V_DIM), BF16),
        compiler_params=_params(),
        name="attention",
    )(q, k, v)


def _rms_kernel(x_ref, g_ref, o_ref):
    x = x_ref[...]
    o_ref[...] = (x * lax.rsqrt(jnp.mean(x * x, axis=-1, keepdims=True) + EPS)) * g_ref[...]


def rms_rows(x, g, *, row0, rows):
    d = x.shape[1]
    bm = _tile(math.gcd(row0, rows) if row0 else rows, 256)
    return pl.pallas_call(
        _rms_kernel,
        grid=(rows // bm,),
        in_specs=[pl.BlockSpec((bm, d), lambda i: (row0 // bm + i, 0)),
                  pl.BlockSpec((1, d), lambda i: (0, 0))],
        out_specs=pl.BlockSpec((bm, d), lambda i: (i, 0)),
        out_shape=jax.ShapeDtypeStruct((rows, d), F32),
        compiler_params=_params(),
        name="final_norm",
    )(x, g.reshape(1, d))


def _ffn_tile_rows(m):
    return _tile(m, 1024)


def _fourier_layer(x, mod, g_mix, g_ffn, w_out, wg, wu, wd, *, m, mp, tp, ts):
    seq = dict(mp=mp, ts=ts)
    h = norm_mod(x, g_mix, mod, 0, 1, packed=False, **seq)
    f = fourier_mix(h, mp=mp, tp=tp, ts=ts)
    x = matmul_residual(f, w_out, x, mod, 2, m=m, name="fnet_out", **seq)
    h = norm_mod(x, g_ffn, mod, 3, 4, packed=False, **seq)
    bm = _ffn_tile_rows(m)
    n_tiles = m // bm
    y = grouped_swiglu(h, wg[None], wu[None], wd[None], jnp.zeros((n_tiles,), jnp.int32),
                       jnp.ones((n_tiles,), jnp.int32), bm=bm)
    return gated_residual(x, mod, 5, y, **seq)


def _attention_layer(x, mod, g_mix, g_ffn, cache_ckv, cache_krope, w_q_a, g_q_a, w_q_b, w_kv_a, g_kv_a,
                     w_kv_b, w_o, w_router, wg, wu, wd, *, m, mp, tp, ts):
    seq = dict(mp=mp, ts=ts)
    nb_p, nb_s = mp // tp, (m - mp) // ts
    past = cache_ckv.shape[1]
    q_rank, kv_rank = w_q_a.shape[1], g_kv_a.shape[0]
    swap = _swap_perm()

    h = norm_mod(x, g_mix, mod, 0, 1, packed=False, **seq)
    w_cat = jnp.concatenate([w_q_a, w_kv_a, w_kv_a[:, kv_rank:][:, swap]], axis=1).astype(BF16)
    qa, ckv, kr = qkv_compress(h, w_cat, g_q_a, g_kv_a, **seq)

    wq = w_q_b.reshape(q_rank, N_HEADS, NOPE_DIM + ROPE_DIM)
    wq_cat = jnp.concatenate([wq, wq[:, :, NOPE_DIM:][:, :, swap]], axis=2).reshape(q_rank, -1)
    q = q_up(qa, wq_cat.astype(BF16), **seq)

    s_len = past + ts
    ckv_s = jnp.concatenate([cache_ckv, ckv[mp:].reshape(nb_s, ts, kv_rank)], axis=1)
    ckv_all = jnp.concatenate([ckv_s.reshape(nb_s * s_len, kv_rank), ckv[:mp]], axis=0).astype(BF16)
    cache_kr = jnp.pad(cache_krope, ((0, 0), (0, 0), (0, ROPE_DIM)))
    kr_s = jnp.concatenate([cache_kr, kr[mp:].reshape(nb_s, ts, 2 * ROPE_DIM)], axis=1)
    kr_all = jnp.concatenate([kr_s.reshape(nb_s * s_len, 2 * ROPE_DIM), kr[:mp]], axis=0).astype(BF16)
    wkv = w_kv_b.reshape(kv_rank, N_HEADS, NOPE_DIM + V_DIM)
    k = head_up(ckv_all, wkv[:, :, :NOPE_DIM].reshape(kv_rank, -1).astype(BF16), kr_all, name="k_up")
    v = matmul(ckv_all, wkv[:, :, NOPE_DIM:].reshape(kv_rank, -1).astype(BF16), BF16,
               bm=512, bn=4096, name="v_up")

    o_p = attention(q, k, v, row0=0, nseq=nb_p, t=tp, s_len=tp, kv_row0=nb_s * s_len,
                    heads=_tile(N_HEADS, 16), tq=_tile(tp, 256))
    o_s = attention(q, k, v, row0=mp, nseq=nb_s, t=ts, s_len=s_len, kv_row0=0,
                    heads=_tile(N_HEADS, 4), tq=_tile(ts, 256))
    x = matmul_residual((o_p, o_s), w_o, x, mod, 2, m=m, name="attn_out", **seq)

    hp = norm_mod(x, g_ffn, mod, 3, 4, packed=True, **seq)
    idx, gates = router_top2(hp, w_router)
    bm = _ffn_tile_rows(m)
    src, pos, tile_expert, tile_valid = moe_dispatch(idx, w_router.shape[1], bm)
    y = grouped_swiglu(gather_unpack(hp, src), wg, wu, wd, tile_expert, tile_valid, bm=bm)
    x = gather_combine(x, mod, 5, y, pos, gates, **seq)
    return x, ckv[:mp], kr[:mp, :ROPE_DIM]


def kernel(x_prompt, x_sample, cache_ckv, cache_krope, c, c_ctx, w_mod, b_mod, g_mix, g_ffn, w_fnet_out, w_ffn_gate, w_ffn_up, w_ffn_down, w_q_a, g_q_a, w_q_b, w_kv_a, g_kv_a, w_kv_b, w_o, w_router, w_exp_gate, w_exp_up, w_exp_down, g_final):
    bp, tp, d = x_prompt.shape
    bs, ts, _ = x_sample.shape
    mp, ms = bp * tp, bs * ts
    depth = w_mod.shape[0]
    assert 1 + bs <= COND_ROWS
    seqs = dict(m=mp + ms, mp=mp, tp=tp, ts=ts)

    x = (x_prompt.reshape(mp, d), x_sample.reshape(ms, d))
    cond = jnp.zeros((COND_ROWS, d), F32).at[0].set(c_ctx).at[1:1 + bs].set(c)

    new_ckv, new_krope = [], []
    for i in range(depth):
        j = i // 2
        mod = adaln(cond, w_mod, b_mod, i)
        if i % 2 == 0:
            x = _fourier_layer(x, mod, g_mix[i], g_ffn[i], w_fnet_out[j], w_ffn_gate[j], w_ffn_up[j],
                               w_ffn_down[j], **seqs)
        else:
            x, ckv_p, kr_p = _attention_layer(
                x, mod, g_mix[i], g_ffn[i], cache_ckv[:, j], cache_krope[:, j], w_q_a[j], g_q_a[j],
                w_q_b[j], w_kv_a[j], g_kv_a[j], w_kv_b[j], w_o[j], w_router[j], w_exp_gate[j],
                w_exp_up[j], w_exp_down[j], **seqs)
            new_ckv.append(ckv_p.reshape(bp, tp, -1))
            new_krope.append(kr_p.reshape(bp, tp, -1))

    y_prompt = rms_rows(x, g_final, row0=0, rows=mp).reshape(bp, tp, d)
    y_sample = rms_rows(x, g_final, row0=mp, rows=ms).reshape(bs, ts, d)
    return (y_prompt, y_sample, jnp.stack(new_ckv, axis=1), jnp.stack(new_krope, axis=1))
```

```python
import functools
import math

import jax
import jax.numpy as jnp
from jax import lax
from jax.experimental import pallas as pl
from jax.experimental.pallas import tpu as pltpu

GRID_W = 64
FNET_GROUPS = 8
N_HEADS = 32
NOPE_DIM = 128
ROPE_DIM = 64
V_DIM = 128
ROPE_THETA = 10000.0
TOP_K = 2
N_MOD = 6
EPS = 1e-6
COND_ROWS = 8
ROUTER_LANES = 128
N_DMA_PRIORITIES = 2
LANES = 128
V7X_VMEM_LIMIT_BYTES = 60 * 1024 * 1024

BF16 = jnp.bfloat16
F32 = jnp.float32


def _params():
    return pltpu.CompilerParams(vmem_limit_bytes=V7X_VMEM_LIMIT_BYTES)


def _tile(n, pref):
    t = min(n, pref)
    while n % t:
        t -= 1
    return t


def _bf16_dot(a, b):
    return jnp.dot(a, b, preferred_element_type=F32)


def _to_bf16(v):
    return v if v.dtype == BF16 else v.astype(BF16)


def _seg_of_tile(i, bm, mp, ts):
    r = i * bm
    return jnp.where(r < mp, 0, 1 + (r - mp) // ts)


def _adaln_kernel(c_ref, w_ref, b_ref, o_ref):
    c = c_ref[...]
    s = c * (1.0 / (1.0 + jnp.exp(-c)))
    o_ref[...] = _bf16_dot(s.astype(BF16), w_ref[...].astype(BF16)) + b_ref[...]


def adaln(cond, w_mod, b_mod, layer):
    r, d = cond.shape
    n = w_mod.shape[-1]
    bn = _tile(n, 1024)
    out = pl.pallas_call(
        _adaln_kernel,
        grid=(n // bn,),
        in_specs=[
            pl.BlockSpec((r, d), lambda j: (0, 0)),
            pl.BlockSpec((None, d, bn), lambda j: (layer, 0, j)),
            pl.BlockSpec((None, 1, bn), lambda j: (layer, 0, j)),
        ],
        out_specs=pl.BlockSpec((r, bn), lambda j: (0, j)),
        out_shape=jax.ShapeDtypeStruct((r, n), F32),
        compiler_params=_params(),
        name="adaln",
    )(cond, w_mod, b_mod.reshape(b_mod.shape[0], 1, n))
    return out.reshape(r * N_MOD, 1, d)


def _pack_bf16_pair(lo, hi):
    lo_bits = lax.bitcast_convert_type(lo.astype(BF16).astype(F32), jnp.uint32)
    hi_bits = lax.bitcast_convert_type(hi.astype(BF16).astype(F32), jnp.uint32)
    return (lo_bits >> 16) | hi_bits


def _unpack_bf16_pair(w):
    lo = lax.bitcast_convert_type(w << 16, F32).astype(BF16)
    hi = lax.bitcast_convert_type(w & jnp.uint32(0xFFFF0000), F32).astype(BF16)
    return lo, hi


def _row_operand(x, bm, bw, npt, col, buffers=None):
    mode = dict(pipeline_mode=pl.Buffered(buffers)) if buffers else {}
    if isinstance(x, tuple):
        return ([pl.BlockSpec((bm, bw), lambda i, *r: (jnp.minimum(i, npt - 1), col(*r)), **mode),
                 pl.BlockSpec((bm, bw), lambda i, *r: (jnp.maximum(i - npt, 0), col(*r)), **mode)],
                list(x))
    return [pl.BlockSpec((bm, bw), lambda i, *r: (i, col(*r)), **mode)], [x]


def _for_row_half(npt, n_variants, body):
    if n_variants == 1:
        body(0)
        return
    i = pl.program_id(0)
    pl.when(i < npt)(lambda: body(0))
    pl.when(i >= npt)(lambda: body(-1))


def _norm_mod_kernel(*refs, n_x, npt, packed):
    x_refs = refs[:n_x]
    g_ref, sh_ref, sc_ref, o_ref = refs[n_x:n_x + 4]

    def body(k):
        x = x_refs[k][...]
        y = x * lax.rsqrt(jnp.mean(x * x, axis=-1, keepdims=True) + EPS)
        y = y * g_ref[...]
        h = y * (1.0 + sc_ref[...]) + sh_ref[...]
        o_ref[...] = h.astype(o_ref.dtype)
        if packed:
            p_ref = refs[-1]
            bm, half = h.shape[0], h.shape[-1] // 2
            ns = half // LANES
            words = _pack_bf16_pair(h[:, :half], h[:, half:])
            for s in range(ns):
                p_ref[pl.ds(s, bm, stride=ns), :] = words[:, s * LANES:(s + 1) * LANES]

    _for_row_half(npt, n_x, body)


def norm_mod(x, g, mod, k_shift, k_scale, *, mp, ts, packed):
    d = g.shape[0]
    m = mp + (x[1].shape[0] if isinstance(x, tuple) else x.shape[0] - mp)
    bm = _tile(math.gcd(mp, ts), 256)
    seg = functools.partial(_seg_of_tile, bm=bm, mp=mp, ts=ts)
    x_specs, x_args = _row_operand(x, bm, d, mp // bm, lambda: 0)
    out_specs = [pl.BlockSpec((bm, d), lambda i: (i, 0))]
    out_shape = [jax.ShapeDtypeStruct((m, d), BF16)]
    if packed:
        ns = d // 2 // LANES
        out_specs.append(pl.BlockSpec((bm * ns, LANES), lambda i: (i, 0)))
        out_shape.append(jax.ShapeDtypeStruct((m * ns, LANES), jnp.uint32))
    outs = pl.pallas_call(
        functools.partial(_norm_mod_kernel, n_x=len(x_args), npt=mp // bm, packed=packed),
        grid=(m // bm,),
        in_specs=x_specs + [
            pl.BlockSpec((1, d), lambda i: (0, 0)),
            pl.BlockSpec((None, 1, d), lambda i: (seg(i) * N_MOD + k_shift, 0, 0)),
            pl.BlockSpec((None, 1, d), lambda i: (seg(i) * N_MOD + k_scale, 0, 0)),
        ],
        out_specs=out_specs,
        out_shape=out_shape,
        compiler_params=_params(),
        name="norm_mod",
    )(*x_args, g.reshape(1, d), mod, mod)
    return tuple(outs) if packed else outs[0]


def _mm_kernel(a_ref, w_ref, o_ref):
    o_ref[...] = _bf16_dot(a_ref[...], _to_bf16(w_ref[...])).astype(o_ref.dtype)


def matmul(a, w, out_dtype, *, bm, bn, name):
    m, k = a.shape
    n = w.shape[1]
    bm, bn = _tile(m, bm), _tile(n, bn)
    return pl.pallas_call(
        _mm_kernel,
        grid=(m // bm, n // bn),
        in_specs=[
            pl.BlockSpec((bm, k), lambda i, j: (i, 0)),
            pl.BlockSpec((k, bn), lambda i, j: (0, j)),
        ],
        out_specs=pl.BlockSpec((bm, bn), lambda i, j: (i, j)),
        out_shape=jax.ShapeDtypeStruct((m, n), out_dtype),
        compiler_params=_params(),
        name=name,
    )(a, w)


def _mm_res_kernel(*refs, n_a, n_res, npt):
    a_refs, w_ref = refs[:n_a], refs[n_a]
    res_refs = refs[n_a + 1:n_a + 1 + n_res]
    gate_ref, o_ref = refs[-2:]

    def body(k):
        acc = _bf16_dot(a_refs[k][...], _to_bf16(w_ref[...]))
        o_ref[...] = res_refs[k][...] + gate_ref[...] * acc

    _for_row_half(npt, max(n_a, n_res), body)


def matmul_residual(a, w, res, mod, k_gate, *, m, mp, ts, name):
    k, n = w.shape
    bm = _tile(math.gcd(mp, ts), 1024)
    bn = _tile(n, 512)
    npt = mp // bm
    seg = functools.partial(_seg_of_tile, bm=bm, mp=mp, ts=ts)
    a_specs, a_args = _row_operand(a, bm, k, npt, lambda j: 0, buffers=1)
    res_specs, res_args = _row_operand(res, bm, bn, npt, lambda j: j)
    return pl.pallas_call(
        functools.partial(_mm_res_kernel, n_a=len(a_args), n_res=len(res_args), npt=npt),
        grid=(m // bm, n // bn),
        in_specs=a_specs + [pl.BlockSpec((k, bn), lambda i, j: (0, j))] + res_specs + [
            pl.BlockSpec((None, 1, bn), lambda i, j: (seg(i) * N_MOD + k_gate, 0, j)),
        ],
        out_specs=pl.BlockSpec((bm, bn), lambda i, j: (i, j)),
        out_shape=jax.ShapeDtypeStruct((m, n), F32),
        compiler_params=_params(),
        name=name,
    )(*a_args, w, *res_args, mod)


def _dft_matrices(n):
    r = _tile(n, 64)
    k = jnp.arange(n, dtype=jnp.int32)

    def cos_sin(rows):
        ang = ((rows[:, None] * k[None, :]) % n).astype(F32) * (2.0 * math.pi / n)
        return jnp.cos(ang), jnp.sin(ang)

    ca, sa = cos_sin(jnp.arange(n // r, dtype=jnp.int32) * r)
    cb, sb = cos_sin(jnp.arange(r, dtype=jnp.int32))
    ca, sa, cb, sb = ca[:, None, :], sa[:, None, :], cb[None, :, :], sb[None, :, :]
    s = n ** -0.5
    return ((ca * cb - sa * sb) * s).reshape(n, n), ((sa * cb + ca * sb) * s).reshape(n, n)


def _dft_channel_kernel(h_ref, w_ref, o_ref):
    c = h_ref.shape[-1]
    y = _bf16_dot(h_ref[...], w_ref[...])
    o_ref[0] = y[:, :c].astype(o_ref.dtype)
    o_ref[1] = y[:, c:].astype(o_ref.dtype)


def dft_channels(h, w_cs, *, row0, nseq, t):
    d = h.shape[1]
    c = d // FNET_GROUPS
    bm = _tile(t, 1024)
    tb = t // bm
    return pl.pallas_call(
        _dft_channel_kernel,
        grid=(nseq * tb, FNET_GROUPS),
        in_specs=[
            pl.BlockSpec((bm, c), lambda i, g: (row0 // bm + i, g)),
            pl.BlockSpec((c, 2 * c), lambda i, g: (0, 0)),
        ],
        out_specs=pl.BlockSpec((None, 2, bm, c), lambda i, g: (i // tb, 0, i % tb, g)),
        out_shape=jax.ShapeDtypeStruct((nseq, 2, t, d), BF16),
        compiler_params=_params(),
        name="dft_channels",
    )(h, w_cs)


def _bmm_kernel(a_ref, b_ref, o_ref):
    o_ref[...] = _bf16_dot(a_ref[...], b_ref[...]).astype(o_ref.dtype)


def dft_positions(a_cs, y):
    nseq, k, d = y.shape
    t = a_cs.shape[0]
    bm, bn = _tile(t, 512), _tile(d, 512)
    tb = t // bm
    return pl.pallas_call(
        _bmm_kernel,
        grid=(nseq, tb, d // bn),
        in_specs=[
            pl.BlockSpec((bm, k), lambda b, i, j: (i, 0)),
            pl.BlockSpec((None, k, bn), lambda b, i, j: (b, 0, j)),
        ],
        out_specs=pl.BlockSpec((bm, bn), lambda b, i, j: (b * tb + i, j)),
        out_shape=jax.ShapeDtypeStruct((nseq * t, d), BF16),
        compiler_params=_params(),
        name="dft_positions",
    )(a_cs, y)


def fourier_mix(h, *, mp, tp, ts):
    m, d = h.shape
    c = d // FNET_GROUPS
    cc, sc = _dft_matrices(c)
    w_cs = jnp.concatenate([cc, sc], axis=1).astype(BF16)
    outs = []
    for row0, rows, t in ((0, mp, tp), (mp, m - mp, ts)):
        nseq = rows // t
        ct, st = _dft_matrices(t)
        a_cs = jnp.concatenate([ct, -st], axis=1).astype(BF16)
        y = dft_channels(h, w_cs, row0=row0, nseq=nseq, t=t)
        outs.append(dft_positions(a_cs, y.reshape(nseq, 2 * t, d)))
    return tuple(outs)


def _ffn_kernel(te_ref, tv_ref, x_ref, wg_ref, wu_ref, wd_ref, o_ref, *, col_chunk):
    del te_ref
    i, f = pl.program_id(0), pl.program_id(1)

    @pl.when(f == 0)
    def _():
        o_ref[...] = jnp.zeros_like(o_ref)

    @pl.when(tv_ref[i] > 0)
    def _():
        x = x_ref[...]
        g = _bf16_dot(x, wg_ref[...].astype(BF16))
        u = _bf16_dot(x, wu_ref[...].astype(BF16))
        h = ((g * (1.0 / (1.0 + jnp.exp(-g)))) * u).astype(BF16)
        for c0 in range(0, o_ref.shape[-1], col_chunk):
            cols = slice(c0, c0 + col_chunk)
            o_ref[:, cols] += _bf16_dot(h, wd_ref[:, cols].astype(BF16))


def grouped_swiglu(x, wg, wu, wd, tile_expert, tile_valid, *, bm):
    p, d = x.shape
    ff = wg.shape[-1]
    bf = _tile(ff, 256)
    nf = ff // bf
    n_tiles = p // bm

    def f_eff(i, f, tv):
        return jnp.where(tv[i] > 0, f, nf - 1)

    grid_spec = pltpu.PrefetchScalarGridSpec(
        num_scalar_prefetch=2,
        grid=(n_tiles, nf),
        in_specs=[
            pl.BlockSpec((bm, d), lambda i, f, te, tv: (i, 0), pipeline_mode=pl.Buffered(1)),
            pl.BlockSpec((None, d, bf), lambda i, f, te, tv: (te[i], 0, f_eff(i, f, tv))),
            pl.BlockSpec((None, d, bf), lambda i, f, te, tv: (te[i], 0, f_eff(i, f, tv))),
            pl.BlockSpec((None, bf, d), lambda i, f, te, tv: (te[i], f_eff(i, f, tv), 0)),
        ],
        out_specs=pl.BlockSpec((bm, d), lambda i, f, te, tv: (i, 0), pipeline_mode=pl.Buffered(1)),
    )
    return pl.pallas_call(
        functools.partial(_ffn_kernel, col_chunk=_tile(d, 512)),
        grid_spec=grid_spec,
        out_shape=jax.ShapeDtypeStruct((p, d), F32),
        compiler_params=_params(),
        name="grouped_swiglu",
    )(tile_expert, tile_valid, x, wg, wu, wd)


def _row_copy(src_ref, dst_ref, sem, src_row, dst_row, span):
    return pltpu.make_async_copy(src_ref.at[pl.ds(src_row * span, span)],
                                 dst_ref.at[pl.ds(dst_row * span, span)], sem)


def _start_row_gather(idx_ref, idx0, idx_stride, src_ref, dst_ref, sem, rows, span=1):
    assert rows % N_DMA_PRIORITIES == 0

    def body(r2, carry):
        for p in range(N_DMA_PRIORITIES):
            r = r2 * N_DMA_PRIORITIES + p
            _row_copy(src_ref, dst_ref, sem, idx_ref[idx0 + r * idx_stride], r, span).start(priority=p)
        return carry
    lax.fori_loop(0, rows // N_DMA_PRIORITIES, body, 0)


def _wait_row_gather(src_ref, dst_ref, sem, rows, span=1):
    def body(r, carry):
        _row_copy(src_ref, dst_ref, sem, 0, r, span).wait()
        return carry
    lax.fori_loop(0, rows, body, 0)


def _double_buffered_step(start, wait):
    i, n = pl.program_id(0), pl.num_programs(0)
    slot = i % 2
    pl.when(i == 0)(lambda: start(i, slot))
    pl.when(i + 1 < n)(lambda: start(i + 1, 1 - slot))
    wait(slot)
    return slot


def _gather_unpack_kernel(idx_ref, src_ref, o_ref, buf_ref, sem, *, rows, ns):
    def start(step, slot):
        _start_row_gather(idx_ref, step * rows, 1, src_ref, buf_ref.at[slot], sem.at[slot], rows, ns)

    def wait(slot):
        _wait_row_gather(src_ref, buf_ref.at[slot], sem.at[slot], rows, ns)

    slot = _double_buffered_step(start, wait)
    lanes = buf_ref.shape[-1]
    half = ns * lanes
    for s in range(ns):
        lo, hi = _unpack_bf16_pair(buf_ref[slot, pl.ds(s, rows, stride=ns), :])
        o_ref[:, s * lanes:(s + 1) * lanes] = lo
        o_ref[:, half + s * lanes:half + (s + 1) * lanes] = hi


def gather_unpack(src, idx, ns):
    n = idx.shape[0]
    lanes = src.shape[1]
    half = ns * lanes
    rows = _tile(n, 512)
    grid_spec = pltpu.PrefetchScalarGridSpec(
        num_scalar_prefetch=1,
        grid=(n // rows,),
        in_specs=[pl.BlockSpec(memory_space=pl.ANY)],
        out_specs=pl.BlockSpec((rows, 2 * half), lambda i, idx: (i, 0)),
        scratch_shapes=[pltpu.VMEM((2, rows * ns, lanes), src.dtype), pltpu.SemaphoreType.DMA((2,))],
    )
    return pl.pallas_call(
        functools.partial(_gather_unpack_kernel, rows=rows, ns=ns),
        grid_spec=grid_spec,
        out_shape=jax.ShapeDtypeStruct((n, 2 * half), BF16),
        compiler_params=_params(),
        name="gather_unpack",
    )(idx, src)


def _gather_combine_kernel(pos_ref, x_ref, gate_ref, w_ref, y_ref, o_ref, buf_ref, sem, *, rows):
    def start(step, slot):
        for k in range(TOP_K):
            _start_row_gather(pos_ref, step * rows * TOP_K + k, TOP_K, y_ref, buf_ref.at[slot, k],
                              sem.at[slot, k], rows)

    def wait(slot):
        for k in range(TOP_K):
            _wait_row_gather(y_ref, buf_ref.at[slot, k], sem.at[slot, k], rows)

    slot = _double_buffered_step(start, wait)
    w = w_ref[...]
    ff = w[:, 0:1] * buf_ref[slot, 0]
    for k in range(1, TOP_K):
        ff = ff + w[:, k:k + 1] * buf_ref[slot, k]
    o_ref[...] = x_ref[...] + gate_ref[...] * ff


def gather_combine(x, mod, k_gate, y, pos, weights, *, mp, ts):
    m, d = x.shape
    rows = _tile(math.gcd(mp, ts), 256)
    seg = functools.partial(_seg_of_tile, bm=rows, mp=mp, ts=ts)
    row_spec = pl.BlockSpec((rows, d), lambda i, pos: (i, 0))
    grid_spec = pltpu.PrefetchScalarGridSpec(
        num_scalar_prefetch=1,
        grid=(m // rows,),
        in_specs=[
            row_spec,
            pl.BlockSpec((None, 1, d), lambda i, pos: (seg(i) * N_MOD + k_gate, 0, 0)),
            pl.BlockSpec((rows, weights.shape[1]), lambda i, pos: (i, 0)),
            pl.BlockSpec(memory_space=pl.ANY),
        ],
        out_specs=row_spec,
        scratch_shapes=[pltpu.VMEM((2, TOP_K, rows, d), F32), pltpu.SemaphoreType.DMA((2, TOP_K))],
    )
    return pl.pallas_call(
        functools.partial(_gather_combine_kernel, rows=rows),
        grid_spec=grid_spec,
        out_shape=jax.ShapeDtypeStruct((m, d), F32),
        compiler_params=_params(),
        name="gather_combine",
    )(pos, x, mod, weights, y)


def _residual_kernel(x_ref, gate_ref, y_ref, o_ref):
    o_ref[...] = x_ref[...] + gate_ref[...] * y_ref[...]


def gated_residual(x, mod, k_gate, y, *, mp, ts):
    m, d = x.shape
    bm = _tile(math.gcd(mp, ts), 256)
    seg = functools.partial(_seg_of_tile, bm=bm, mp=mp, ts=ts)
    row_spec = pl.BlockSpec((bm, d), lambda i: (i, 0))
    return pl.pallas_call(
        _residual_kernel,
        grid=(m // bm,),
        in_specs=[row_spec, pl.BlockSpec((None, 1, d), lambda i: (seg(i) * N_MOD + k_gate, 0, 0)),
                  row_spec],
        out_specs=row_spec,
        out_shape=jax.ShapeDtypeStruct((m, d), F32),
        compiler_params=_params(),
        name="gated_residual",
    )(x, mod, y)


def _router_kernel(x_ref, w_ref, idx_ref, gate_ref, *, n_experts):
    logits = _bf16_dot(x_ref[...], w_ref[...])
    lane = lax.broadcasted_iota(jnp.int32, logits.shape, 1)
    neg = jnp.float32(-jnp.inf)
    lg = jnp.where(lane < n_experts, logits, neg)
    m1 = jnp.max(lg, axis=-1, keepdims=True)
    i1 = jnp.min(jnp.where(lg == m1, lane, ROUTER_LANES), axis=-1, keepdims=True)
    lg2 = jnp.where(lane == i1, neg, lg)
    m2 = jnp.max(lg2, axis=-1, keepdims=True)
    i2 = jnp.min(jnp.where(lg2 == m2, lane, ROUTER_LANES), axis=-1, keepdims=True)
    e = jnp.exp(m2 - m1)
    denom = 1.0 + e
    idx_ref[...] = jnp.where(lane == 0, i1, jnp.where(lane == 1, i2, 0))
    gate_ref[...] = jnp.where(lane == 0, 1.0 / denom, jnp.where(lane == 1, e / denom, 0.0))


def router_top2(h, w_router):
    m, d = h.shape
    n_experts = w_router.shape[1]
    w = jnp.zeros((d, ROUTER_LANES), BF16).at[:, :n_experts].set(w_router.astype(BF16))
    bm = _tile(m, 1024)
    out_spec = pl.BlockSpec((bm, ROUTER_LANES), lambda i: (i, 0))
    return pl.pallas_call(
        functools.partial(_router_kernel, n_experts=n_experts),
        grid=(m // bm,),
        in_specs=[pl.BlockSpec((bm, d), lambda i: (i, 0)),
                  pl.BlockSpec((d, ROUTER_LANES), lambda i: (0, 0))],
        out_specs=[out_spec, out_spec],
        out_shape=[jax.ShapeDtypeStruct((m, ROUTER_LANES), jnp.int32),
                   jax.ShapeDtypeStruct((m, ROUTER_LANES), F32)],
        compiler_params=_params(),
        name="router_top2",
    )(h, w)


def moe_dispatch(idx, n_experts, bm):
    m = idx.shape[0]
    flat_e = idx[:, :TOP_K].reshape(-1)
    onehot = (flat_e[:, None] == jnp.arange(n_experts, dtype=jnp.int32)[None, :]).astype(jnp.int32)
    csum = jnp.cumsum(onehot, axis=0)
    rank = jnp.take_along_axis(csum, flat_e[:, None], axis=1)[:, 0] - 1
    tiles_e = (csum[-1] + bm - 1) // bm
    tile_end = jnp.cumsum(tiles_e)
    tile_start = tile_end - tiles_e
    pos = tile_start[flat_e] * bm + rank
    n_tiles = (m * TOP_K) // bm + n_experts
    src = jnp.zeros((n_tiles * bm,), jnp.int32).at[pos].set(
        jnp.arange(m * TOP_K, dtype=jnp.int32) // TOP_K)
    tile_ids = jnp.arange(n_tiles, dtype=jnp.int32)
    total = tile_end[-1]
    expert_of = lambda t: jnp.sum((tile_end[None, :] <= t[:, None]).astype(jnp.int32), axis=1)
    tile_expert = jnp.minimum(expert_of(tile_ids), expert_of((total - 1)[None])[0])
    tile_valid = (tile_ids < total).astype(jnp.int32)
    return src, pos, tile_expert, tile_valid


def _rope_mix_tables(n_tokens, bm, lead):
    rows = n_tokens // GRID_W
    row = jnp.repeat(jnp.arange(rows, dtype=F32), GRID_W)
    col = jnp.tile(jnp.arange(GRID_W, dtype=F32), rows)
    half = ROPE_DIM // 2
    inv = ROPE_THETA ** (-jnp.arange(0, half, 2, dtype=F32) / half)
    ang_r = row[:, None] * inv
    ang_c = col[:, None] * inv
    cos = jnp.concatenate([jnp.cos(ang_r)] * 2 + [jnp.cos(ang_c)] * 2, axis=1)
    sin = jnp.concatenate([-jnp.sin(ang_r), jnp.sin(ang_r), -jnp.sin(ang_c), jnp.sin(ang_c)], axis=1)
    cos = jnp.concatenate([cos, jnp.ones((bm, ROPE_DIM), F32)], axis=0)
    sin = jnp.concatenate([sin, jnp.zeros((bm, ROPE_DIM), F32)], axis=0)
    n = n_tokens + bm
    mul = jnp.concatenate([jnp.ones((n, lead), F32), cos, jnp.zeros((n, ROPE_DIM), F32)], axis=1)
    add = jnp.concatenate([jnp.zeros((n, lead), F32), sin, jnp.zeros((n, ROPE_DIM), F32)], axis=1)
    return mul, add


def _rope_mix(z, mul, add):
    return z * mul + pltpu.roll(z, z.shape[-1] - ROPE_DIM, 1) * add


def _rope_table_spec(bm, width, npt, tps):
    return pl.BlockSpec((bm, width), lambda i, *_: (jnp.where(i < npt, tps, (i - npt) % tps), 0))


def _swap_perm():
    q = ROPE_DIM // 4
    return jnp.concatenate([jnp.arange(q, 2 * q), jnp.arange(0, q),
                            jnp.arange(3 * q, 4 * q), jnp.arange(2 * q, 3 * q)])


def _qkv_kernel(h_ref, w_ref, gq_ref, gkv_ref, mul_ref, add_ref, qa_ref, ckv_ref, kr_ref,
                *, q_rank, kv_rank):
    z = _bf16_dot(h_ref[...], w_ref[...])

    def rms(v, g):
        return (v * lax.rsqrt(jnp.mean(v * v, axis=-1, keepdims=True) + EPS)) * g

    qa_ref[...] = rms(z[:, :q_rank], gq_ref[...]).astype(qa_ref.dtype)
    ckv_ref[...] = rms(z[:, q_rank:q_rank + kv_rank], gkv_ref[...])
    kr_ref[...] = _rope_mix(z[:, q_rank + kv_rank:], mul_ref[...], add_ref[...])


def qkv_compress(h, w_cat, g_q, g_kv, *, mp, ts):
    m, d = h.shape
    q_rank, kv_rank = g_q.shape[0], g_kv.shape[0]
    n = w_cat.shape[1]
    bm = _tile(math.gcd(mp, ts), 512)
    mul, add = _rope_mix_tables(ts, bm, 0)
    tab_spec = _rope_table_spec(bm, 2 * ROPE_DIM, mp // bm, ts // bm)
    return pl.pallas_call(
        functools.partial(_qkv_kernel, q_rank=q_rank, kv_rank=kv_rank),
        grid=(m // bm,),
        in_specs=[
            pl.BlockSpec((bm, d), lambda i: (i, 0)),
            pl.BlockSpec((d, n), lambda i: (0, 0)),
            pl.BlockSpec((1, q_rank), lambda i: (0, 0)),
            pl.BlockSpec((1, kv_rank), lambda i: (0, 0)),
            tab_spec, tab_spec,
        ],
        out_specs=[
            pl.BlockSpec((bm, q_rank), lambda i: (i, 0)),
            pl.BlockSpec((bm, kv_rank), lambda i: (i, 0)),
            pl.BlockSpec((bm, 2 * ROPE_DIM), lambda i: (i, 0)),
        ],
        out_shape=[
            jax.ShapeDtypeStruct((m, q_rank), BF16),
            jax.ShapeDtypeStruct((m, kv_rank), F32),
            jax.ShapeDtypeStruct((m, 2 * ROPE_DIM), F32),
        ],
        compiler_params=_params(),
        name="qkv_compress",
    )(h, w_cat, g_q.reshape(1, -1), g_kv.reshape(1, -1), mul, add)


def _q_up_kernel(a_ref, w_ref, mul_ref, add_ref, o_ref):
    acc = _bf16_dot(a_ref[...], w_ref[...])
    qw = mul_ref.shape[-1]
    for c0 in range(0, acc.shape[-1], qw):
        cols = slice(c0, c0 + qw)
        o_ref[:, cols] = _rope_mix(acc[:, cols], mul_ref[...], add_ref[...]).astype(o_ref.dtype)


def q_up(qa, wq_cat, *, mp, ts):
    m, k = qa.shape
    n = wq_cat.shape[1]
    qw = NOPE_DIM + 2 * ROPE_DIM
    bm = _tile(math.gcd(mp, ts), 512)
    bn = qw * _tile(n // qw, 8)
    mul, add = _rope_mix_tables(ts, bm, NOPE_DIM)
    tab_spec = _rope_table_spec(bm, qw, mp // bm, ts // bm)
    return pl.pallas_call(
        _q_up_kernel,
        grid=(m // bm, n // bn),
        in_specs=[pl.BlockSpec((bm, k), lambda i, j: (i, 0)),
                  pl.BlockSpec((k, bn), lambda i, j: (0, j)),
                  tab_spec, tab_spec],
        out_specs=pl.BlockSpec((bm, bn), lambda i, j: (i, j)),
        out_shape=jax.ShapeDtypeStruct((m, n), BF16),
        compiler_params=_params(),
        name="q_up",
    )(qa, wq_cat, mul, add)


def _head_up_kernel(a_ref, w_ref, fill_ref, o_ref, *, cw):
    acc = _bf16_dot(a_ref[...], w_ref[...])
    fill = fill_ref[...]
    ow = cw + fill.shape[-1]
    for hh in range(acc.shape[-1] // cw):
        o_ref[:, hh * ow:hh * ow + cw] = acc[:, hh * cw:(hh + 1) * cw].astype(o_ref.dtype)
        o_ref[:, hh * ow + cw:(hh + 1) * ow] = fill


def head_up(ckv, w, fill, *, name):
    m, k = ckv.shape
    cw = w.shape[1] // N_HEADS
    ow = cw + fill.shape[1]
    bm = _tile(m, 512)
    hb = _tile(N_HEADS, 16)
    return pl.pallas_call(
        functools.partial(_head_up_kernel, cw=cw),
        grid=(m // bm, N_HEADS // hb),
        in_specs=[pl.BlockSpec((bm, k), lambda i, j: (i, 0)),
                  pl.BlockSpec((k, hb * cw), lambda i, j: (0, j)),
                  pl.BlockSpec((bm, fill.shape[1]), lambda i, j: (i, 0))],
        out_specs=pl.BlockSpec((bm, hb * ow), lambda i, j: (i, j)),
        out_shape=jax.ShapeDtypeStruct((m, N_HEADS * ow), BF16),
        compiler_params=_params(),
        name=name,
    )(ckv, w, fill)


def _attn_kernel(q_ref, k_ref, v_ref, o_ref, *, heads, exp2_scale):
    qw = q_ref.shape[-1] // heads
    vw = v_ref.shape[-1] // heads
    contract_last = (((1,), (1,)), ((), ()))
    for hh in range(heads):
        q = q_ref[:, hh * qw:(hh + 1) * qw]
        k = k_ref[:, hh * qw:(hh + 1) * qw]
        v = v_ref[:, hh * vw:(hh + 1) * vw]
        s = lax.dot_general(q, k, contract_last, preferred_element_type=F32)
        p = jnp.exp2((s - jnp.max(s, axis=-1, keepdims=True)) * exp2_scale)
        l = jnp.sum(p, axis=-1, keepdims=True)
        o = _bf16_dot(p.astype(BF16), v) / l
        o_ref[:, hh * V_DIM:(hh + 1) * V_DIM] = o.astype(o_ref.dtype)


def attention(q, k, v, *, row0, nseq, t, s_len, kv_row0, heads, tq):
    assert kv_row0 % s_len == 0 and row0 % tq == 0
    qw = NOPE_DIM + 2 * ROPE_DIM
    vw = v.shape[1] // N_HEADS
    tb = t // tq
    scale = (NOPE_DIM + ROPE_DIM) ** -0.5
    kv_map = lambda b, h, i: (kv_row0 // s_len + b, h)
    return pl.pallas_call(
        functools.partial(_attn_kernel, heads=heads, exp2_scale=float(scale * math.log2(math.e))),
        grid=(nseq, N_HEADS // heads, tb),
        in_specs=[pl.BlockSpec((tq, heads * qw), lambda b, h, i: (row0 // tq + b * tb + i, h)),
                  pl.BlockSpec((s_len, heads * qw), kv_map),
                  pl.BlockSpec((s_len, heads * vw), kv_map)],
        out_specs=pl.BlockSpec((tq, heads * V_DIM), lambda b, h, i: (b * tb + i, h)),
        out_shape=jax.ShapeDtypeStruct((nseq * t, N_HEADS * V_DIM), BF16),
        compiler_params=_params(),
        name="attention",
    )(q, k, v)


def _rms_kernel(x_ref, g_ref, o_ref):
    x = x_ref[...]
    o_ref[...] = (x * lax.rsqrt(jnp.mean(x * x, axis=-1, keepdims=True) + EPS)) * g_ref[...]


def rms_rows(x, g, *, row0, rows):
    d = x.shape[1]
    bm = _tile(math.gcd(row0, rows) if row0 else rows, 256)
    return pl.pallas_call(
        _rms_kernel,
        grid=(rows // bm,),
        in_specs=[pl.BlockSpec((bm, d), lambda i: (row0 // bm + i, 0)),
                  pl.BlockSpec((1, d), lambda i: (0, 0))],
        out_specs=pl.BlockSpec((bm, d), lambda i: (i, 0)),
        out_shape=jax.ShapeDtypeStruct((rows, d), F32),
        compiler_params=_params(),
        name="final_norm",
    )(x, g.reshape(1, d))


def _ffn_tile_rows(m):
    return _tile(m, 1024)


def _fourier_layer(x, mod, g_mix, g_ffn, w_out, wg, wu, wd, *, m, mp, tp, ts):
    seq = dict(mp=mp, ts=ts)
    h = norm_mod(x, g_mix, mod, 0, 1, packed=False, **seq)
    f = fourier_mix(h, mp=mp, tp=tp, ts=ts)
    x = matmul_residual(f, w_out, x, mod, 2, m=m, name="fnet_out", **seq)
    h = norm_mod(x, g_ffn, mod, 3, 4, packed=False, **seq)
    bm = _ffn_tile_rows(m)
    n_tiles = m // bm
    y = grouped_swiglu(h, wg[None], wu[None], wd[None], jnp.zeros((n_tiles,), jnp.int32),
                       jnp.ones((n_tiles,), jnp.int32), bm=bm)
    return gated_residual(x, mod, 5, y, **seq)


def _attention_layer(x, mod, g_mix, g_ffn, cache_ckv, cache_krope, w_q_a, g_q_a, w_q_b, w_kv_a, g_kv_a,
                     w_kv_b, w_o, w_router, wg, wu, wd, *, m, mp, tp, ts):
    seq = dict(mp=mp, ts=ts)
    nb_p, nb_s = mp // tp, (m - mp) // ts
    past = cache_ckv.shape[1]
    q_rank, kv_rank = w_q_a.shape[1], g_kv_a.shape[0]
    swap = _swap_perm()

    h = norm_mod(x, g_mix, mod, 0, 1, packed=False, **seq)
    w_cat = jnp.concatenate([w_q_a, w_kv_a, w_kv_a[:, kv_rank:][:, swap]], axis=1).astype(BF16)
    qa, ckv, kr = qkv_compress(h, w_cat, g_q_a, g_kv_a, **seq)

    wq = w_q_b.reshape(q_rank, N_HEADS, NOPE_DIM + ROPE_DIM)
    wq_cat = jnp.concatenate([wq, wq[:, :, NOPE_DIM:][:, :, swap]], axis=2).reshape(q_rank, -1)
    q = q_up(qa, wq_cat.astype(BF16), **seq)

    s_len = past + ts
    ckv_s = jnp.concatenate([cache_ckv, ckv[mp:].reshape(nb_s, ts, kv_rank)], axis=1)
    ckv_all = jnp.concatenate([ckv_s.reshape(nb_s * s_len, kv_rank), ckv[:mp]], axis=0).astype(BF16)
    cache_kr = jnp.pad(cache_krope, ((0, 0), (0, 0), (0, ROPE_DIM)))
    kr_s = jnp.concatenate([cache_kr, kr[mp:].reshape(nb_s, ts, 2 * ROPE_DIM)], axis=1)
    kr_all = jnp.concatenate([kr_s.reshape(nb_s * s_len, 2 * ROPE_DIM), kr[:mp]], axis=0).astype(BF16)
    wkv = w_kv_b.reshape(kv_rank, N_HEADS, NOPE_DIM + V_DIM)
    k = head_up(ckv_all, wkv[:, :, :NOPE_DIM].reshape(kv_rank, -1).astype(BF16), kr_all, name="k_up")
    v = matmul(ckv_all, wkv[:, :, NOPE_DIM:].reshape(kv_rank, -1).astype(BF16), BF16,
               bm=512, bn=4096, name="v_up")

    o_p = attention(q, k, v, row0=0, nseq=nb_p, t=tp, s_len=tp, kv_row0=nb_s * s_len,
                    heads=_tile(N_HEADS, 16), tq=_tile(tp, 256))
    o_s = attention(q, k, v, row0=mp, nseq=nb_s, t=ts, s_len=s_len, kv_row0=0,
                    heads=_tile(N_HEADS, 4), tq=_tile(ts, 256))
    x = matmul_residual((o_p, o_s), w_o, x, mod, 2, m=m, name="attn_out", **seq)

    h, hp = norm_mod(x, g_ffn, mod, 3, 4, packed=True, **seq)
    idx, gates = router_top2(h, w_router)
    bm = _ffn_tile_rows(m)
    src, pos, tile_expert, tile_valid = moe_dispatch(idx, w_router.shape[1], bm)
    x_sorted = gather_unpack(hp, src, hp.shape[0] // m)
    y = grouped_swiglu(x_sorted, wg, wu, wd, tile_expert, tile_valid, bm=bm)
    x = gather_combine(x, mod, 5, y, pos, gates, **seq)
    return x, ckv[:mp], kr[:mp, :ROPE_DIM]


def kernel(x_prompt, x_sample, cache_ckv, cache_krope, c, c_ctx, w_mod, b_mod, g_mix, g_ffn, w_fnet_out, w_ffn_gate, w_ffn_up, w_ffn_down, w_q_a, g_q_a, w_q_b, w_kv_a, g_kv_a, w_kv_b, w_o, w_router, w_exp_gate, w_exp_up, w_exp_down, g_final):
    bp, tp, d = x_prompt.shape
    bs, ts, _ = x_sample.shape
    mp, ms = bp * tp, bs * ts
    depth = w_mod.shape[0]
    assert 1 + bs <= COND_ROWS
    seqs = dict(m=mp + ms, mp=mp, tp=tp, ts=ts)

    x = (x_prompt.reshape(mp, d), x_sample.reshape(ms, d))
    cond = jnp.zeros((COND_ROWS, d), F32).at[0].set(c_ctx).at[1:1 + bs].set(c)

    new_ckv, new_krope = [], []
    for i in range(depth):
        j = i // 2
        mod = adaln(cond, w_mod, b_mod, i)
        if i % 2 == 0:
            x = _fourier_layer(x, mod, g_mix[i], g_ffn[i], w_fnet_out[j], w_ffn_gate[j], w_ffn_up[j],
                               w_ffn_down[j], **seqs)
        else:
            x, ckv_p, kr_p = _attention_layer(
                x, mod, g_mix[i], g_ffn[i], cache_ckv[:, j], cache_krope[:, j], w_q_a[j], g_q_a[j],
                w_q_b[j], w_kv_a[j], g_kv_a[j], w_kv_b[j], w_o[j], w_router[j], w_exp_gate[j],
                w_exp_up[j], w_exp_down[j], **seqs)
            new_ckv.append(ckv_p.reshape(bp, tp, -1))
            new_krope.append(kr_p.reshape(bp, tp, -1))

    y_prompt = rms_rows(x, g_final, row0=0, rows=mp).reshape(bp, tp, d)
    y_sample = rms_rows(x, g_final, row0=mp, rows=ms).reshape(bs, ts, d)
    return (y_prompt, y_sample, jnp.stack(new_ckv, axis=1), jnp.stack(new_krope, axis=1))
```
